```python
import math
import jax
import jax.numpy as jnp
from jax import lax
import numpy as np

D_MODEL = 2048
BATCH = 2
SEQ = 8192
DEPTH = 2

D_MIX = D_MODEL
N_MIXERS = 4
GROUP_WIDTH = D_MIX // N_MIXERS

S5_CH_PER_GROUP = 16
S5_GROUPS = GROUP_WIDTH // S5_CH_PER_GROUP
S5_STATE = 64
S5_STEP_MIN = 1e-3
S5_STEP_MAX = 1e-1

SGU_CHUNK = 128
SGU_HEADS = 8
SGU_HEAD_DIM = GROUP_WIDTH // SGU_HEADS

POOL_WINDOWS = (2, 4, 8, 16)
POOL_GROUPS = len(POOL_WINDOWS)
POOL_GROUP_DIM = GROUP_WIDTH // POOL_GROUPS

DN_HEAD_DIM = 128
DN_HEADS = GROUP_WIDTH // DN_HEAD_DIM
DN_CONV = 4
DN_CHUNK = 64
DN_DT_MIN = 1e-3
DN_DT_MAX = 1e-1

D_FF = 4 * D_MODEL

LN_EPS = 1e-5
RMS_EPS = 1e-6
L2_EPS = 1e-6
DEEPNORM_ALPHA = (2 * DEPTH) ** 0.25
DEEPNORM_BETA = (8 * DEPTH) ** -0.25

S5_OFF = 0
SGU_OFF = S5_OFF + GROUP_WIDTH
POOL_OFF = SGU_OFF + 2 * GROUP_WIDTH
DN_QKV_OFF = POOL_OFF + GROUP_WIDTH
DN_GATE_OFF = DN_QKV_OFF + 3 * GROUP_WIDTH
DN_A_OFF = DN_GATE_OFF + GROUP_WIDTH
DN_B_OFF = DN_A_OFF + DN_HEADS
IN_COLS = DN_B_OFF + DN_HEADS

kernel_name = 'hybrid_s5_sgu_pool_gdn_deepnorm'


def _layer_norm(x, g, b):
    xf = x.astype(jnp.float32)
    mu = jnp.mean(xf, axis=-1, keepdims=True)
    var = jnp.mean(jnp.square(xf - mu), axis=-1, keepdims=True)
    return (xf - mu) * lax.rsqrt(var + LN_EPS) * g.astype(jnp.float32) + b.astype(jnp.float32)


def _rms_norm(x, g):
    return x * lax.rsqrt(jnp.mean(jnp.square(x), axis=-1, keepdims=True) + RMS_EPS) * g.astype(jnp.float32)


def _l2_normalize(x):
    return x * lax.rsqrt(jnp.sum(jnp.square(x), axis=-1, keepdims=True) + L2_EPS)


def _linear_recurrence_combine(left, right):
    a_l, b_l = left
    a_r, b_r = right
    return a_l * a_r, a_r * b_l + b_r


def _s5_mixer(u, lam_re, lam_im, log_step, b_re, b_im, c_re, c_im, d, glu_w, glu_b):
    bsz, l, _ = u.shape
    uf = u.astype(jnp.float32).reshape(bsz, l, S5_GROUPS, S5_CH_PER_GROUP)
    lam = lax.complex(lam_re.astype(jnp.float32), lam_im.astype(jnp.float32))
    step = jnp.exp(log_step.astype(jnp.float32))[:, None]
    lam_bar = jnp.exp(lam * step)
    b_mat = lax.complex(b_re.astype(jnp.float32), b_im.astype(jnp.float32))
    b_bar = ((lam_bar - 1.0) / lam)[:, :, None] * b_mat
    bu = jnp.einsum('blgh,gph->blgp', uf.astype(jnp.complex64), b_bar)
    a = jnp.broadcast_to(lam_bar, bu.shape)
    _, states = lax.associative_scan(_linear_recurrence_combine, (a, bu), axis=1)
    c_mat = lax.complex(c_re.astype(jnp.float32), c_im.astype(jnp.float32))
    y = jnp.real(jnp.einsum('blgp,ghp->blgh', states, c_mat)) + d.astype(jnp.float32) * uf
    y = jax.nn.gelu(y.reshape(bsz, l, GROUP_WIDTH))
    return y * jax.nn.sigmoid(y @ glu_w.astype(jnp.float32) + glu_b.astype(jnp.float32))


def _sgu_mixer(z, norm_g, norm_b, w_s, b_s):
    z = jax.nn.gelu(z.astype(jnp.float32))
    u, v = jnp.split(z, 2, axis=-1)
    v = _layer_norm(v, norm_g, norm_b)
    bsz, l, _ = v.shape
    n = l // SGU_CHUNK
    v = v.reshape(bsz, n, SGU_CHUNK, SGU_HEADS, SGU_HEAD_DIM)
    causal = jnp.tril(jnp.ones((SGU_CHUNK, SGU_CHUNK), dtype=bool))
    w = jnp.where(causal, w_s.astype(jnp.float32), 0.0)
    bias = b_s.astype(jnp.float32).T[None, None, :, :, None]
    mixed = jnp.einsum('hts,bnshd->bnthd', w, v) + bias
    return u * mixed.reshape(bsz, l, GROUP_WIDTH)


def _pool_mixer(p, w_pool, scale):
    pf = p.astype(jnp.float32)
    bsz, l, _ = pf.shape
    groups = pf.reshape(bsz, l, POOL_GROUPS, POOL_GROUP_DIM)
    csum = jnp.cumsum(groups, axis=1)
    pos = jnp.arange(l)
    outs = []
    for gi, win in enumerate(POOL_WINDOWS):
        cs = csum[:, :, gi]
        prev = jnp.pad(cs, ((0, 0), (win, 0), (0, 0)))[:, :l]
        count = jnp.minimum(pos + 1, win).astype(jnp.float32)[None, :, None]
        outs.append((cs - prev) / count - groups[:, :, gi])
    pooled = jnp.stack(outs, axis=2)
    mixed = jnp.einsum('blgc,gcd->blgd', pooled, w_pool.astype(jnp.float32))
    return mixed.reshape(bsz, l, GROUP_WIDTH) * scale.astype(jnp.float32)


def _chunk_gated_delta_rule(q, k, v, g, beta):
    bsz, l, h, dk = q.shape
    dv = v.shape[-1]
    c = DN_CHUNK
    n = l // c

    def chunk4(t):
        return t.reshape(bsz, n, c, h, t.shape[-1]).transpose(0, 3, 1, 2, 4)

    def chunk3(t):
        return t.reshape(bsz, n, c, h).transpose(0, 3, 1, 2)

    q, k, v = chunk4(q), chunk4(k), chunk4(v)
    g, beta = chunk3(g), chunk3(beta)
    gc = jnp.cumsum(g, axis=-1)
    causal = jnp.tril(jnp.ones((c, c), dtype=bool))
    strict = jnp.tril(jnp.ones((c, c), dtype=bool), k=-1)
    decay = jnp.exp(jnp.where(causal, gc[..., :, None] - gc[..., None, :], -jnp.inf))
    k_beta = k * beta[..., None]
    v_beta = v * beta[..., None]
    eye = jnp.eye(c, dtype=jnp.float32)
    kk = jnp.einsum('bhncd,bhnsd->bhncs', k_beta, k) * decay
    a_mat = jnp.where(strict, kk, 0.0) + eye
    t_mat = lax.linalg.triangular_solve(a_mat, jnp.broadcast_to(eye, a_mat.shape),
                                        left_side=True, lower=True, unit_diagonal=True)
    u = jnp.einsum('bhncs,bhnsd->bhncd', t_mat, v_beta)
    w = jnp.einsum('bhncs,bhnsd->bhncd', t_mat, k_beta * jnp.exp(gc)[..., None])
    qk = jnp.einsum('bhncd,bhnsd->bhncs', q, k) * decay
    q_g = q * jnp.exp(gc)[..., None]
    k_tail = k * jnp.exp(gc[..., -1:] - gc)[..., None]
    g_last = jnp.exp(gc[..., -1])

    def step(state, inp):
        u_c, w_c, qg_c, qk_c, kt_c, gl_c = inp
        v_new = u_c - jnp.einsum('bhck,bhkv->bhcv', w_c, state)
        o_c = jnp.einsum('bhck,bhkv->bhcv', qg_c, state) + jnp.einsum('bhcs,bhsv->bhcv', qk_c, v_new)
        state = state * gl_c[..., None, None] + jnp.einsum('bhck,bhcv->bhkv', kt_c, v_new)
        return state, o_c

    xs = (jnp.moveaxis(u, 2, 0), jnp.moveaxis(w, 2, 0), jnp.moveaxis(q_g, 2, 0),
          jnp.moveaxis(qk, 2, 0), jnp.moveaxis(k_tail, 2, 0), jnp.moveaxis(g_last, 2, 0))
    state0 = jnp.zeros((bsz, h, dk, dv), jnp.float32)
    _, o = lax.scan(step, state0, xs)
    return o.transpose(1, 0, 3, 2, 4).reshape(bsz, l, h, dv)


def _deltanet_mixer(qkv, gate, a_logit, b_logit, conv_w, a_log, dt_bias, norm_g):
    bsz, l, _ = qkv.shape
    x3 = qkv.astype(jnp.float32)
    conv = lax.conv_general_dilated(x3, conv_w.astype(jnp.float32)[:, None, :], window_strides=(1,),
                                    padding=[(DN_CONV - 1, 0)], dimension_numbers=('NWC', 'WIO', 'NWC'),
                                    feature_group_count=3 * GROUP_WIDTH)
    x3 = jax.nn.silu(conv)
    q, k, v = jnp.split(x3, 3, axis=-1)
    q = _l2_normalize(q.reshape(bsz, l, DN_HEADS, DN_HEAD_DIM)) * (DN_HEAD_DIM ** -0.5)
    k = _l2_normalize(k.reshape(bsz, l, DN_HEADS, DN_HEAD_DIM))
    v = v.reshape(bsz, l, DN_HEADS, DN_HEAD_DIM)
    g = -jnp.exp(a_log.astype(jnp.float32)) * jax.nn.softplus(a_logit.astype(jnp.float32) + dt_bias.astype(jnp.float32))
    beta = jax.nn.sigmoid(b_logit.astype(jnp.float32))
    o = _chunk_gated_delta_rule(q, k, v, g, beta)
    gate = gate.astype(jnp.float32).reshape(bsz, l, DN_HEADS, DN_HEAD_DIM)
    o = _rms_norm(o, norm_g) * jax.nn.silu(gate)
    return o.reshape(bsz, l, GROUP_WIDTH)


def setup_inputs(seed: int = 0) -> dict:
    key = jax.random.key(seed)
    ks = jax.random.split(key, 32)
    f32 = jnp.float32
    nl = DEPTH

    def nrm(k, shape, scale):
        return jax.random.normal(k, shape, f32) * scale

    x = nrm(ks[0], (BATCH, SEQ, D_MODEL), 1.0)
    w_in = nrm(ks[1], (nl, D_MODEL, IN_COLS), D_MODEL ** -0.5)
    n_idx = jnp.arange(S5_STATE, dtype=f32)
    s5_lambda_re = -0.5 + nrm(ks[2], (nl, S5_GROUPS, S5_STATE), 0.01)
    s5_lambda_im = math.pi * n_idx + nrm(ks[3], (nl, S5_GROUPS, S5_STATE), 0.01)
    s5_log_step = jax.random.uniform(ks[4], (nl, S5_GROUPS), f32, math.log(S5_STEP_MIN), math.log(S5_STEP_MAX))
    s5_b_re = nrm(ks[5], (nl, S5_GROUPS, S5_STATE, S5_CH_PER_GROUP), (2 * S5_CH_PER_GROUP) ** -0.5)
    s5_b_im = nrm(ks[6], (nl, S5_GROUPS, S5_STATE, S5_CH_PER_GROUP), (2 * S5_CH_PER_GROUP) ** -0.5)
    s5_c_re = nrm(ks[7], (nl, S5_GROUPS, S5_CH_PER_GROUP, S5_STATE), (2 * S5_STATE) ** -0.5)
    s5_c_im = nrm(ks[8], (nl, S5_GROUPS, S5_CH_PER_GROUP, S5_STATE), (2 * S5_STATE) ** -0.5)
    s5_d = nrm(ks[9], (nl, S5_GROUPS, S5_CH_PER_GROUP), 1.0)
    s5_glu_w = nrm(ks[10], (nl, GROUP_WIDTH, GROUP_WIDTH), GROUP_WIDTH ** -0.5)
    s5_glu_b = nrm(ks[11], (nl, GROUP_WIDTH), 0.01)
    sgu_norm_g = 1.0 + nrm(ks[12], (nl, GROUP_WIDTH), 0.02)
    sgu_norm_b = nrm(ks[13], (nl, GROUP_WIDTH), 0.02)
    sgu_w = nrm(ks[14], (nl, SGU_HEADS, SGU_CHUNK, SGU_CHUNK), SGU_CHUNK ** -0.5)
    sgu_b = 1.0 + nrm(ks[15], (nl, SGU_HEADS, SGU_CHUNK), 0.02)
    pool_w = nrm(ks[16], (nl, POOL_GROUPS, POOL_GROUP_DIM, POOL_GROUP_DIM), POOL_GROUP_DIM ** -0.5)
    pool_scale = 1.0 + nrm(ks[17], (nl, GROUP_WIDTH), 0.02)
    dn_conv_w = nrm(ks[18], (nl, DN_CONV, 3 * GROUP_WIDTH), DN_CONV ** -0.5)
    dn_a_log = jnp.log(jax.random.uniform(ks[19], (nl, DN_HEADS), f32, 1.0, 16.0))
    dt = jnp.exp(jax.random.uniform(ks[20], (nl, DN_HEADS), f32, math.log(DN_DT_MIN), math.log(DN_DT_MAX)))
    dn_dt_bias = dt + jnp.log(-jnp.expm1(-dt))
    dn_norm_g = 1.0 + nrm(ks[21], (nl, DN_HEAD_DIM), 0.02)
    w_out = nrm(ks[22], (nl, D_MIX, D_MODEL), D_MIX ** -0.5) * DEEPNORM_BETA
    ln1_g = 1.0 + nrm(ks[23], (nl, D_MODEL), 0.02)
    ln1_b = nrm(ks[24], (nl, D_MODEL), 0.02)
    w_up = nrm(ks[25], (nl, D_MODEL, D_FF), D_MODEL ** -0.5)
    w_down = nrm(ks[26], (nl, D_FF, D_MODEL), D_FF ** -0.5) * DEEPNORM_BETA
    ln2_g = 1.0 + nrm(ks[27], (nl, D_MODEL), 0.02)
    ln2_b = nrm(ks[28], (nl, D_MODEL), 0.02)
    return {'x': x, 'w_in': w_in, 's5_lambda_re': s5_lambda_re, 's5_lambda_im': s5_lambda_im,
            's5_log_step': s5_log_step, 's5_b_re': s5_b_re, 's5_b_im': s5_b_im, 's5_c_re': s5_c_re,
            's5_c_im': s5_c_im, 's5_d': s5_d, 's5_glu_w': s5_glu_w, 's5_glu_b': s5_glu_b,
            'sgu_norm_g': sgu_norm_g, 'sgu_norm_b': sgu_norm_b, 'sgu_w': sgu_w, 'sgu_b': sgu_b,
            'pool_w': pool_w, 'pool_scale': pool_scale, 'dn_conv_w': dn_conv_w, 'dn_a_log': dn_a_log,
            'dn_dt_bias': dn_dt_bias, 'dn_norm_g': dn_norm_g, 'w_out': w_out, 'ln1_g': ln1_g,
            'ln1_b': ln1_b, 'w_up': w_up, 'w_down': w_down, 'ln2_g': ln2_g, 'ln2_b': ln2_b}


def reference(x, w_in, s5_lambda_re, s5_lambda_im, s5_log_step, s5_b_re, s5_b_im, s5_c_re, s5_c_im,
              s5_d, s5_glu_w, s5_glu_b, sgu_norm_g, sgu_norm_b, sgu_w, sgu_b, pool_w, pool_scale,
              dn_conv_w, dn_a_log, dn_dt_bias, dn_norm_g, w_out, ln1_g, ln1_b, w_up, w_down,
              ln2_g, ln2_b):
    for i in range(DEPTH):
        proj = x @ w_in[i]
        y_s5 = _s5_mixer(proj[..., S5_OFF:SGU_OFF], s5_lambda_re[i], s5_lambda_im[i], s5_log_step[i],
                         s5_b_re[i], s5_b_im[i], s5_c_re[i], s5_c_im[i], s5_d[i], s5_glu_w[i], s5_glu_b[i])
        y_sgu = _sgu_mixer(proj[..., SGU_OFF:POOL_OFF], sgu_norm_g[i], sgu_norm_b[i], sgu_w[i], sgu_b[i])
        y_pool = _pool_mixer(proj[..., POOL_OFF:DN_QKV_OFF], pool_w[i], pool_scale[i])
        y_dn = _deltanet_mixer(proj[..., DN_QKV_OFF:DN_GATE_OFF], proj[..., DN_GATE_OFF:DN_A_OFF],
                               proj[..., DN_A_OFF:DN_B_OFF], proj[..., DN_B_OFF:IN_COLS],
                               dn_conv_w[i], dn_a_log[i], dn_dt_bias[i], dn_norm_g[i])
        mixed = jnp.concatenate([y_s5, y_sgu, y_pool, y_dn], axis=-1).astype(x.dtype)
        x = _layer_norm(DEEPNORM_ALPHA * x + mixed @ w_out[i], ln1_g[i], ln1_b[i]).astype(x.dtype)
        hidden = jnp.square(jax.nn.relu(x @ w_up[i]))
        x = _layer_norm(DEEPNORM_ALPHA * x + hidden @ w_down[i], ln2_g[i], ln2_b[i]).astype(x.dtype)
    return x
```

```python
import functools
import math

import jax
import jax.numpy as jnp
from jax import lax
from jax.experimental import pallas as pl
from jax.experimental.pallas import tpu as pltpu

F32 = jnp.float32
BF16 = jnp.bfloat16
HIGHEST = lax.Precision.HIGHEST

LANES = 128
SUBLANES = 8

S5_CH_PER_GROUP = 16
S5_STATE = 64
S5_GROUPS_PER_SLAB = LANES // S5_CH_PER_GROUP
S5_SLAB_STATE = S5_GROUPS_PER_SLAB * S5_STATE
SGU_CHUNK = 128
SGU_HEAD_DIM = 64
POOL_WINDOWS = (2, 4, 8, 16)
POOL_HALO = 16
DN_HEAD_DIM = 128
DN_CONV = 4
DN_CHUNK = 64
DN_HALO = 8
LN_EPS = 1e-5
RMS_EPS = 1e-6
L2_EPS = 1e-6


def _params(semantics, vmem_mib):
    return pltpu.CompilerParams(dimension_semantics=semantics, vmem_limit_bytes=vmem_mib * 1024 * 1024)


def _layer_norm(y, g, b):
    mu = jnp.mean(y, axis=-1, keepdims=True)
    yc = y - mu
    var = jnp.mean(yc * yc, axis=-1, keepdims=True)
    return yc * lax.rsqrt(var + LN_EPS) * g + b


def _dot(a, b):
    return jnp.dot(a, b, preferred_element_type=F32)


def _dot_nt(a, b, precision=None):
    return lax.dot_general(a, b, (((1,), (1,)), ((), ())), precision=precision, preferred_element_type=F32)


def _dot_tn(a, b):
    return lax.dot_general(a, b, (((0,), (0,)), ((), ())), preferred_element_type=F32)


def _proj_body(x_ref, w_ref, wt_ref, o_ref, ot_ref):
    xb = x_ref[...]
    o_ref[...] = _dot(xb, w_ref[...])

    @pl.when(pl.program_id(1) == 0)
    def _():
        ot_ref[...] = _dot(xb, wt_ref[...])


def _proj(xb, w_main, w_tail, tm=1024, tn=512):
    n, d = xb.shape
    cols = w_main.shape[1]
    tm = min(tm, n)
    return pl.pallas_call(
        _proj_body,
        grid=(n // tm, cols // tn),
        in_specs=[pl.BlockSpec((tm, d), lambda i, j: (i, 0)),
                  pl.BlockSpec((d, tn), lambda i, j: (0, j)),
                  pl.BlockSpec((d, LANES), lambda i, j: (0, 0))],
        out_specs=[pl.BlockSpec((tm, tn), lambda i, j: (i, j)),
                   pl.BlockSpec((tm, LANES), lambda i, j: (i, 0))],
        out_shape=[jax.ShapeDtypeStruct((n, cols), F32), jax.ShapeDtypeStruct((n, LANES), F32)],
        compiler_params=_params(("parallel", "arbitrary"), 40),
        name="proj",
    )(xb, w_main, w_tail)


def _s5_body(u_ref, wb_ref, wc_ref, are_ref, aim_ref, pre_ref, pim_ref, d_ref, gw_ref, gb_ref, o_ref,
             sre, sim, hre, him, y_scr, *, tl, steps):
    n_slabs = u_ref.shape[1] // LANES
    sw = S5_SLAB_STATE

    @pl.when(pl.program_id(1) == 0)
    def _():
        hre[...] = jnp.zeros_like(hre)
        him[...] = jnp.zeros_like(him)

    u = u_ref[...]
    ub = u.astype(BF16)
    tiles = sw // LANES
    for j in range(n_slabs):
        bu = _dot(ub[:, j * LANES:(j + 1) * LANES], wb_ref[j])
        for k in range(tiles):
            t = j * tiles + k
            cols = slice(t * LANES, (t + 1) * LANES)
            sre[t] = bu[:, k * LANES:(k + 1) * LANES]
            sim[t] = bu[:, sw + k * LANES:sw + (k + 1) * LANES]

            ar = jnp.broadcast_to(are_ref[:, cols], (SUBLANES, LANES))
            ai = jnp.broadcast_to(aim_ref[:, cols], (SUBLANES, LANES))
            hr = jnp.zeros((SUBLANES, LANES), F32)
            hi = jnp.zeros((SUBLANES, LANES), F32)
            for r in range(steps):
                rows = pl.ds(r, SUBLANES, stride=steps)
                hr, hi = ar * hr - ai * hi + sre[t, rows, :], ar * hi + ai * hr + sim[t, rows, :]
                sre[t, rows, :] = hr
                sim[t, rows, :] = hi

            apr = pre_ref[steps - 1:steps, cols]
            api = pim_ref[steps - 1:steps, cols]
            cr = hre[:, cols]
            ci = him[:, cols]
            crs, cis = [], []
            for s in range(SUBLANES):
                crs.append(cr)
                cis.append(ci)
                cr, ci = (apr * cr - api * ci + hr[s:s + 1, :], apr * ci + api * cr + hi[s:s + 1, :])
            hre[:, cols] = cr
            him[:, cols] = ci
            cr8 = jnp.concatenate(crs, axis=0)
            ci8 = jnp.concatenate(cis, axis=0)
            for r in range(steps):
                rows = pl.ds(r, SUBLANES, stride=steps)
                pr = jnp.broadcast_to(pre_ref[r:r + 1, cols], (SUBLANES, LANES))
                pi = jnp.broadcast_to(pim_ref[r:r + 1, cols], (SUBLANES, LANES))
                sre[t, rows, :] = sre[t, rows, :] + (pr * cr8 - pi * ci8)
                sim[t, rows, :] = sim[t, rows, :] + (pr * ci8 + pi * cr8)

        h_re = jnp.concatenate([sre[j * tiles + k] for k in range(tiles)], axis=1).astype(BF16)
        h_im = jnp.concatenate([sim[j * tiles + k] for k in range(tiles)], axis=1).astype(BF16)
        y = _dot(h_re, wc_ref[j, :sw, :]) + _dot(h_im, wc_ref[j, sw:, :])
        lc = slice(j * LANES, (j + 1) * LANES)
        y = y + d_ref[:, lc] * u[:, lc]
        y_scr[:, lc] = jax.nn.gelu(y)

    y = y_scr[...]
    gate = _dot(y.astype(BF16), gw_ref[...]) + gb_ref[...]
    o_ref[...] = (y * jax.nn.sigmoid(gate)).astype(o_ref.dtype)


def _s5_prepare(lam_re, lam_im, log_step, b_re, b_im, c_re, c_im, d, steps):
    g, p = lam_re.shape
    h = S5_CH_PER_GROUP
    q = S5_GROUPS_PER_SLAB
    n_slabs = g // q
    step = jnp.exp(log_step.astype(F32))[:, None]
    lr, li = lam_re.astype(F32), lam_im.astype(F32)
    mag = jnp.exp(lr * step)
    a_re, a_im = mag * jnp.cos(li * step), mag * jnp.sin(li * step)
    den = lr * lr + li * li
    f_re = ((a_re - 1.0) * lr + a_im * li) / den
    f_im = (a_im * lr - (a_re - 1.0) * li) / den
    bb_re = f_re[:, :, None] * b_re - f_im[:, :, None] * b_im
    bb_im = f_re[:, :, None] * b_im + f_im[:, :, None] * b_re
    eye = jnp.eye(q, dtype=F32)

    def in_block(m):
        m = m.reshape(n_slabs, q, p, h)
        return jnp.einsum('jqph,qr->jqhrp', m, eye).reshape(n_slabs, q * h, q * p)

    def out_block(m):
        m = m.reshape(n_slabs, q, h, p)
        return jnp.einsum('jqhp,qr->jqprh', m, eye).reshape(n_slabs, q * p, q * h)

    wb = jnp.concatenate([in_block(bb_re), in_block(bb_im)], axis=2).astype(BF16)
    wc = jnp.concatenate([out_block(c_re.astype(F32)), out_block(-c_im.astype(F32))], axis=1).astype(BF16)
    k = jnp.arange(1, steps + 1, dtype=F32)[:, None, None]
    pmag = jnp.exp(lr * step * k)
    ang = li * step * k
    pw_re = (pmag * jnp.cos(ang)).reshape(steps, g * p)
    pw_im = (pmag * jnp.sin(ang)).reshape(steps, g * p)
    return (wb, wc, a_re.reshape(1, g * p), a_im.reshape(1, g * p), pw_re, pw_im,
            d.astype(F32).reshape(1, g * h))


def _s5(proj, bsz, seq, prep, glu_w, glu_b, tl=256):
    wb, wc, a_re, a_im, pw_re, pw_im, d = prep
    tl = min(tl, seq)
    steps = tl // SUBLANES
    width = d.shape[1]
    states = pw_re.shape[1]
    nl = seq // tl
    full = lambda a: pl.BlockSpec(a.shape, lambda b, l: (0,) * a.ndim)
    gw = glu_w.astype(BF16)
    gb = glu_b.astype(F32).reshape(1, width)
    return pl.pallas_call(
        functools.partial(_s5_body, tl=tl, steps=steps),
        grid=(bsz, nl),
        in_specs=[pl.BlockSpec((tl, width), lambda b, l: (b * nl + l, 0)),
                  full(wb), full(wc), full(a_re), full(a_im), full(pw_re), full(pw_im), full(d), full(gw), full(gb)],
        out_specs=pl.BlockSpec((tl, width), lambda b, l: (b * nl + l, 0)),
        out_shape=jax.ShapeDtypeStruct((bsz * seq, width), BF16),
        scratch_shapes=[pltpu.VMEM((states // LANES, tl, LANES), F32), pltpu.VMEM((states // LANES, tl, LANES), F32),
                        pltpu.VMEM((1, states), F32), pltpu.VMEM((1, states), F32),
                        pltpu.VMEM((tl, width), F32)],
        compiler_params=_params(("arbitrary", "arbitrary"), 40),
        name="s5",
    )(proj, wb, wc, a_re, a_im, pw_re, pw_im, d, gw, gb)


def _sgu_body(zu_ref, zv_ref, g_ref, b_ref, w_ref, bias_ref, o_ref, *, tl):
    width = zu_ref.shape[1]
    u = jax.nn.gelu(zu_ref[...])
    v = _layer_norm(jax.nn.gelu(zv_ref[...]), g_ref[...], b_ref[...])
    vb = v.astype(BF16)
    row = lax.broadcasted_iota(jnp.int32, (2 * SGU_CHUNK, SGU_CHUNK), 0)
    col = lax.broadcasted_iota(jnp.int32, (2 * SGU_CHUNK, SGU_CHUNK), 1)
    causal = col <= (row % SGU_CHUNK)
    lane = lax.broadcasted_iota(jnp.int32, (SGU_CHUNK, LANES), 1)
    first = lane < SGU_HEAD_DIM
    for j in range(width // LANES):
        w = jnp.where(causal, w_ref[j], 0.0).astype(BF16)
        bias = bias_ref[j]
        for c in range(tl // SGU_CHUNK):
            rs = slice(c * SGU_CHUNK, (c + 1) * SGU_CHUNK)
            cs = slice(j * LANES, (j + 1) * LANES)
            r = _dot(w, vb[rs, cs])
            mixed = jnp.where(first, r[:SGU_CHUNK], r[SGU_CHUNK:]) + bias
            o_ref[rs, cs] = (u[rs, cs] * mixed).astype(o_ref.dtype)


def _sgu(proj, n, width, norm_g, norm_b, w_s, b_s, tl=512):
    heads = w_s.shape[0]
    tl = min(tl, n)
    w_pairs = w_s.astype(F32).reshape(heads // 2, 2 * SGU_CHUNK, SGU_CHUNK)
    bias = jnp.repeat(b_s.astype(F32).T, SGU_HEAD_DIM, axis=1)
    bias = bias.reshape(SGU_CHUNK, heads // 2, LANES).transpose(1, 0, 2)
    g = norm_g.astype(F32).reshape(1, width)
    b = norm_b.astype(F32).reshape(1, width)
    full = lambda a: pl.BlockSpec(a.shape, lambda i: (0,) * a.ndim)
    return pl.pallas_call(
        functools.partial(_sgu_body, tl=tl),
        grid=(n // tl,),
        in_specs=[pl.BlockSpec((tl, width), lambda i: (i, 1)),
                  pl.BlockSpec((tl, width), lambda i: (i, 2)),
                  full(g), full(b), full(w_pairs), full(bias)],
        out_specs=pl.BlockSpec((tl, width), lambda i: (i, 0)),
        out_shape=jax.ShapeDtypeStruct((n, width), BF16),
        compiler_params=_params(("parallel",), 40),
        name="sgu",
    )(proj, proj, g, b, w_pairs, bias)


def _pool_body(p_ref, w_ref, sc_ref, o_ref, ext, *, tl):
    l = pl.program_id(1)

    @pl.when(l == 0)
    def _():
        ext[0:POOL_HALO, :] = jnp.zeros((POOL_HALO, ext.shape[1]), F32)

    @pl.when(l > 0)
    def _():
        ext[0:POOL_HALO, :] = ext[tl:tl + POOL_HALO, :]

    ext[POOL_HALO:POOL_HALO + tl, :] = p_ref[...]
    pos = l * tl + lax.broadcasted_iota(jnp.int32, (tl, LANES), 0)
    for gi, win in enumerate(POOL_WINDOWS):
        cs = slice(gi * LANES, (gi + 1) * LANES)
        x = ext[POOL_HALO:POOL_HALO + tl, cs]
        acc = x
        for j in range(1, win):
            acc = acc + ext[POOL_HALO - j:POOL_HALO - j + tl, cs]
        count = jnp.minimum(pos + 1, win).astype(F32)
        pooled = acc / count - x
        y = _dot(pooled.astype(BF16), w_ref[gi]) * sc_ref[:, cs]
        o_ref[:, cs] = y.astype(o_ref.dtype)


def _pool(proj, bsz, seq, width, w_pool, scale, col_block, tl=512):
    tl = min(tl, seq)
    nl = seq // tl
    w = w_pool.astype(BF16)
    sc = scale.astype(F32).reshape(1, width)
    return pl.pallas_call(
        functools.partial(_pool_body, tl=tl),
        grid=(bsz, nl),
        in_specs=[pl.BlockSpec((tl, width), lambda b, l: (b * nl + l, col_block)),
                  pl.BlockSpec(w.shape, lambda b, l: (0, 0, 0)),
                  pl.BlockSpec(sc.shape, lambda b, l: (0, 0))],
        out_specs=pl.BlockSpec((tl, width), lambda b, l: (b * nl + l, 0)),
        out_shape=jax.ShapeDtypeStruct((bsz * seq, width), BF16),
        scratch_shapes=[pltpu.VMEM((tl + POOL_HALO, width), F32)],
        compiler_params=_params(("arbitrary", "arbitrary"), 40),
        name="pool",
    )(proj, w, sc)


def _unit_lower_inverse(low):
    c = low.shape[0]
    eye = (lax.broadcasted_iota(jnp.int32, (c, c), 0) == lax.broadcasted_iota(jnp.int32, (c, c), 1)).astype(F32)
    inv = eye - low
    power = low
    span = 2
    while span < c:
        power = jnp.dot(power, power, precision=HIGHEST, preferred_element_type=F32)
        inv = inv + jnp.dot(inv, power, precision=HIGHEST, preferred_element_type=F32)
        span *= 2
    return inv


def _dn_body(q_ref, k_ref, v_ref, gate_ref, tail_ref, cw_ref, alog_ref, dtb_ref, ng_ref, o_ref,
             ext, state, *, tl, heads):
    l = pl.program_id(1)
    width = heads * DN_HEAD_DIM
    c = DN_CHUNK

    @pl.when(l == 0)
    def _():
        ext[:, 0:DN_HALO, :] = jnp.zeros((3, DN_HALO, width), F32)
        state[...] = jnp.zeros_like(state)

    @pl.when(l > 0)
    def _():
        ext[:, 0:DN_HALO, :] = ext[:, tl:tl + DN_HALO, :]

    mixed = []
    for i, ref in enumerate((q_ref, k_ref, v_ref)):
        ext[i, DN_HALO:DN_HALO + tl, :] = ref[...]
        acc = jnp.zeros((tl, width), F32)
        for j in range(DN_CONV):
            off = DN_HALO - (DN_CONV - 1) + j
            acc = acc + ext[i, off:off + tl, :] * cw_ref[j:j + 1, i * width:(i + 1) * width]
        mixed.append(acc * jax.nn.sigmoid(acc))
    qc, kc, vc = mixed

    tail = tail_ref[...]
    sp_in = tail + dtb_ref[...]
    softplus = jnp.maximum(sp_in, 0.0) + jnp.log1p(jnp.exp(-jnp.abs(sp_in)))
    g_all = -jnp.exp(alog_ref[...]) * softplus
    beta_all = jax.nn.sigmoid(tail)

    ri = lax.broadcasted_iota(jnp.int32, (c, c), 0)
    ci = lax.broadcasted_iota(jnp.int32, (c, c), 1)
    causal = ri >= ci
    strict = ri > ci
    tri = causal.astype(F32)
    sel_lane = lax.broadcasted_iota(jnp.int32, (c, LANES), 1)
    gate = gate_ref[...]

    for n in range(tl // c):
        rs = slice(n * c, (n + 1) * c)
        gc_all = jnp.dot(tri, g_all[rs, :], precision=HIGHEST, preferred_element_type=F32)
        for h in range(heads):
            hs = slice(h * DN_HEAD_DIM, (h + 1) * DN_HEAD_DIM)
            q = qc[rs, hs]
            k = kc[rs, hs]
            v = vc[rs, hs]
            q = q * lax.rsqrt(jnp.sum(q * q, axis=-1, keepdims=True) + L2_EPS) * (DN_HEAD_DIM ** -0.5)
            k = k * lax.rsqrt(jnp.sum(k * k, axis=-1, keepdims=True) + L2_EPS)
            beta = beta_all[rs, heads + h:heads + h + 1]
            gcol = gc_all[:, h:h + 1]
            glast = gc_all[c - 1:c, h:h + 1]
            sel = (sel_lane == h).astype(F32)
            grow = _dot_nt(sel, gc_all, precision=HIGHEST)
            decay = jnp.exp(jnp.where(causal, gcol - grow, -jnp.inf))
            kb = k * beta
            vb = v * beta
            kbf = k.astype(BF16)
            kk = _dot_nt(kb.astype(BF16), kbf) * decay
            t_mat = _unit_lower_inverse(jnp.where(strict, kk, 0.0)).astype(BF16)
            egc = jnp.exp(gcol)
            u = _dot(t_mat, vb.astype(BF16))
            w = _dot(t_mat, (kb * egc).astype(BF16))
            qk = (_dot_nt(q.astype(BF16), kbf) * decay).astype(BF16)
            qg = (q * egc).astype(BF16)
            kt = (k * jnp.exp(glast - gcol)).astype(BF16)
            s = state[h]
            sb = s.astype(BF16)
            v_new = u - _dot(w.astype(BF16), sb)
            vnb = v_new.astype(BF16)
            o = _dot(qg, sb) + _dot(qk, vnb)
            state[h] = s * jnp.exp(glast) + _dot_tn(kt, vnb)
            o = o * lax.rsqrt(jnp.mean(o * o, axis=-1, keepdims=True) + RMS_EPS) * ng_ref[...]
            gt = gate[rs, hs]
            o_ref[rs, hs] = (o * (gt * jax.nn.sigmoid(gt))).astype(o_ref.dtype)


def _dn(proj, tail, bsz, seq, width, col_block, conv_w, a_log, dt_bias, norm_g, tl=128):
    heads = width // DN_HEAD_DIM
    tl = min(tl, seq)
    nl = seq // tl
    cw = conv_w.astype(F32)
    pad = lambda a: jnp.pad(a.astype(F32), (0, LANES - heads)).reshape(1, LANES)
    alog = pad(a_log)
    dtb = pad(dt_bias)
    ng = norm_g.astype(F32).reshape(1, DN_HEAD_DIM)
    full = lambda a: pl.BlockSpec(a.shape, lambda b, l: (0,) * a.ndim)
    slab = lambda k: pl.BlockSpec((tl, width), lambda b, l: (b * nl + l, col_block + k))
    return pl.pallas_call(
        functools.partial(_dn_body, tl=tl, heads=heads),
        grid=(bsz, nl),
        in_specs=[slab(0), slab(1), slab(2), slab(3),
                  pl.BlockSpec((tl, LANES), lambda b, l: (b * nl + l, 0)),
                  full(cw), full(alog), full(dtb), full(ng)],
        out_specs=pl.BlockSpec((tl, width), lambda b, l: (b * nl + l, 0)),
        out_shape=jax.ShapeDtypeStruct((bsz * seq, width), BF16),
        scratch_shapes=[pltpu.VMEM((3, tl + DN_HALO, width), F32),
                        pltpu.VMEM((heads, DN_HEAD_DIM, DN_HEAD_DIM), F32)],
        compiler_params=_params(("arbitrary", "arbitrary"), 40),
        name="deltanet",
    )(proj, proj, proj, proj, tail, cw, alog, dtb, ng)


def _out_body(y0_ref, y1_ref, y2_ref, y3_ref, w_ref, x_ref, g_ref, b_ref, o_ref, ob_ref, *, alpha):
    width = y0_ref.shape[1]
    acc = alpha * x_ref[...]
    for i, ref in enumerate((y0_ref, y1_ref, y2_ref, y3_ref)):
        acc = acc + _dot(ref[...], w_ref[i * width:(i + 1) * width, :])
    y = _layer_norm(acc, g_ref[...], b_ref[...])
    o_ref[...] = y
    ob_ref[...] = y.astype(BF16)


def _out_proj(ys, w_out, x, g, b, alpha, tm=512):
    n, d = x.shape
    width = ys[0].shape[1]
    tm = min(tm, n)
    row = lambda wd: pl.BlockSpec((tm, wd), lambda i: (i, 0))
    full = lambda a: pl.BlockSpec(a.shape, lambda i: (0,) * a.ndim)
    g = g.astype(F32).reshape(1, d)
    b = b.astype(F32).reshape(1, d)
    return pl.pallas_call(
        functools.partial(_out_body, alpha=alpha),
        grid=(n // tm,),
        in_specs=[row(width)] * 4 + [full(w_out), row(d), full(g), full(b)],
        out_specs=[row(d), row(d)],
        out_shape=[jax.ShapeDtypeStruct((n, d), F32), jax.ShapeDtypeStruct((n, d), BF16)],
        compiler_params=_params(("parallel",), 56),
        name="out_proj_ln",
    )(*ys, w_out, x, g, b)


def _ffn_body(xb_ref, wu_ref, wd_ref, x_ref, g_ref, b_ref, o_ref, ob_ref, acc, *, alpha):
    f = pl.program_id(1)

    @pl.when(f == 0)
    def _():
        acc[...] = alpha * x_ref[...]

    hidden = jnp.maximum(_dot(xb_ref[...], wu_ref[...]), 0.0)
    acc[...] += _dot((hidden * hidden).astype(BF16), wd_ref[...])

    @pl.when(f == pl.num_programs(1) - 1)
    def _():
        y = _layer_norm(acc[...], g_ref[...], b_ref[...])
        o_ref[...] = y
        ob_ref[...] = y.astype(BF16)


def _ffn(xb, x, w_up, w_down, g, b, alpha, tm=512, tf=512):
    n, d = x.shape
    ff = w_up.shape[1]
    tm = min(tm, n)
    g = g.astype(F32).reshape(1, d)
    b = b.astype(F32).reshape(1, d)
    row = pl.BlockSpec((tm, d), lambda i, f: (i, 0))
    vec = pl.BlockSpec((1, d), lambda i, f: (0, 0))
    return pl.pallas_call(
        functools.partial(_ffn_body, alpha=alpha),
        grid=(n // tm, ff // tf),
        in_specs=[row, pl.BlockSpec((d, tf), lambda i, f: (0, f)), pl.BlockSpec((tf, d), lambda i, f: (f, 0)),
                  row, vec, vec],
        out_specs=[row, row],
        out_shape=[jax.ShapeDtypeStruct((n, d), F32), jax.ShapeDtypeStruct((n, d), BF16)],
        scratch_shapes=[pltpu.VMEM((tm, d), F32)],
        compiler_params=_params(("parallel", "arbitrary"), 56),
        name="ffn_ln",
    )(xb, w_up, w_down, x, g, b)


def kernel(x, w_in, s5_lambda_re, s5_lambda_im, s5_log_step, s5_b_re, s5_b_im, s5_c_re, s5_c_im, s5_d, s5_glu_w, s5_glu_b, sgu_norm_g, sgu_norm_b, sgu_w, sgu_b, pool_w, pool_scale, dn_conv_w, dn_a_log, dn_dt_bias, dn_norm_g, w_out, ln1_g, ln1_b, w_up, w_down, ln2_g, ln2_b):
    bsz, seq, d = x.shape
    depth = w_in.shape[0]
    n = bsz * seq
    width = s5_glu_w.shape[1]
    heads = dn_a_log.shape[1]
    main_cols = 8 * width
    alpha = (2 * depth) ** 0.25
    s5_tl = min(256, seq)

    xf = x.reshape(n, d).astype(F32)
    xb = xf.astype(BF16)
    for i in range(depth):
        w_main = w_in[i, :, :main_cols].astype(BF16)
        w_tail = jnp.pad(w_in[i, :, main_cols:], ((0, 0), (0, LANES - 2 * heads))).astype(BF16)
        proj, tail = _proj(xb, w_main, w_tail)
        prep = _s5_prepare(s5_lambda_re[i], s5_lambda_im[i], s5_log_step[i], s5_b_re[i], s5_b_im[i],
                           s5_c_re[i], s5_c_im[i], s5_d[i], s5_tl // SUBLANES)
        y_s5 = _s5(proj, bsz, seq, prep, s5_glu_w[i], s5_glu_b[i], tl=s5_tl)
        y_sgu = _sgu(proj, n, width, sgu_norm_g[i], sgu_norm_b[i], sgu_w[i], sgu_b[i])
        y_pool = _pool(proj, bsz, seq, width, pool_w[i], pool_scale[i], col_block=3)
        y_dn = _dn(proj, tail, bsz, seq, width, 4, dn_conv_w[i], dn_a_log[i], dn_dt_bias[i], dn_norm_g[i])
        xf, xb = _out_proj((y_s5, y_sgu, y_pool, y_dn), w_out[i].astype(BF16), xf, ln1_g[i], ln1_b[i], alpha)
        xf, xb = _ffn(xb, xf, w_up[i].astype(BF16), w_down[i].astype(BF16), ln2_g[i], ln2_b[i], alpha)
    return xf.reshape(bsz, seq, d).astype(x.dtype)
```

```python
import functools
import math

import jax
import jax.numpy as jnp
from jax import lax
from jax.experimental import pallas as pl
from jax.experimental.pallas import tpu as pltpu

F32 = jnp.float32
BF16 = jnp.bfloat16
HIGHEST = lax.Precision.HIGHEST

LANES = 128
SUBLANES = 8

S5_CH_PER_GROUP = 16
S5_STATE = 64
S5_GROUPS_PER_SLAB = LANES // S5_CH_PER_GROUP
S5_SLAB_STATE = S5_GROUPS_PER_SLAB * S5_STATE
SGU_CHUNK = 128
SGU_HEAD_DIM = 64
POOL_WINDOWS = (2, 4, 8, 16)
POOL_HALO = 16
DN_HEAD_DIM = 128
DN_CONV = 4
DN_CHUNK = 64
DN_HALO = 8
LN_EPS = 1e-5
RMS_EPS = 1e-6
L2_EPS = 1e-6


def _params(semantics, vmem_mib):
    return pltpu.CompilerParams(dimension_semantics=semantics, vmem_limit_bytes=vmem_mib * 1024 * 1024)


def _layer_norm(y, g, b):
    mu = jnp.mean(y, axis=-1, keepdims=True)
    yc = y - mu
    var = jnp.mean(yc * yc, axis=-1, keepdims=True)
    return yc * lax.rsqrt(var + LN_EPS) * g + b


def _dot(a, b):
    return jnp.dot(a, b, preferred_element_type=F32)


def _dot_nt(a, b, precision=None):
    return lax.dot_general(a, b, (((1,), (1,)), ((), ())), precision=precision, preferred_element_type=F32)


def _dot_tn(a, b):
    return lax.dot_general(a, b, (((0,), (0,)), ((), ())), preferred_element_type=F32)


def _proj_body(x_ref, w_ref, wt_ref, o_ref, ot_ref):
    xb = x_ref[...]
    o_ref[...] = _dot(xb, w_ref[...])

    @pl.when(pl.program_id(1) == 0)
    def _():
        ot_ref[...] = _dot(xb, wt_ref[...])


def _proj(xb, w_main, w_tail, tm=1024, tn=512):
    n, d = xb.shape
    cols = w_main.shape[1]
    tm = min(tm, n)
    return pl.pallas_call(
        _proj_body,
        grid=(n // tm, cols // tn),
        in_specs=[pl.BlockSpec((tm, d), lambda i, j: (i, 0)),
                  pl.BlockSpec((d, tn), lambda i, j: (0, j)),
                  pl.BlockSpec((d, LANES), lambda i, j: (0, 0))],
        out_specs=[pl.BlockSpec((tm, tn), lambda i, j: (i, j)),
                   pl.BlockSpec((tm, LANES), lambda i, j: (i, 0))],
        out_shape=[jax.ShapeDtypeStruct((n, cols), F32), jax.ShapeDtypeStruct((n, LANES), F32)],
        compiler_params=_params(("parallel", "arbitrary"), 40),
        name="proj",
    )(xb, w_main, w_tail)


def _s5_body(u_ref, perm_ref, unperm_ref, wb_ref, wc_ref, are_ref, aim_ref, pre_ref, pim_ref, d_ref, gw_ref, gb_ref,
             o_ref, sre, sim, hre, him, y_scr, *, tl, steps):
    n_slabs = u_ref.shape[1] // LANES
    sw = S5_SLAB_STATE
    tiles = sw // LANES

    @pl.when(pl.program_id(1) == 0)
    def _():
        hre[...] = jnp.zeros_like(hre)
        him[...] = jnp.zeros_like(him)

    u_nat = u_ref[...]
    perm = perm_ref[...]
    u1 = u_nat.astype(BF16)
    rest = u_nat - u1.astype(F32)
    u2 = rest.astype(BF16)
    u3 = (rest - u2.astype(F32)).astype(BF16)
    u = _dot(perm, u1) + _dot(perm, u2) + _dot(perm, u3)
    ub = u.astype(BF16)

    for j in range(n_slabs):
        bu = _dot(ub[:, j * LANES:(j + 1) * LANES], wb_ref[j])
        ts = [j * tiles + k for k in range(tiles)]
        cols = [slice(t * LANES, (t + 1) * LANES) for t in ts]
        for k, t in enumerate(ts):
            sre[t] = bu[:, k * LANES:(k + 1) * LANES]
            sim[t] = bu[:, sw + k * LANES:sw + (k + 1) * LANES]

        ar = [jnp.broadcast_to(are_ref[:, c], (SUBLANES, LANES)) for c in cols]
        ai = [jnp.broadcast_to(aim_ref[:, c], (SUBLANES, LANES)) for c in cols]
        hr = [jnp.zeros((SUBLANES, LANES), F32)] * tiles
        hi = [jnp.zeros((SUBLANES, LANES), F32)] * tiles
        for r in range(steps):
            rows = slice(r * SUBLANES, (r + 1) * SUBLANES)
            for k, t in enumerate(ts):
                hr[k], hi[k] = (ar[k] * hr[k] - ai[k] * hi[k] + sre[t, rows, :],
                                ar[k] * hi[k] + ai[k] * hr[k] + sim[t, rows, :])
                sre[t, rows, :] = hr[k]
                sim[t, rows, :] = hi[k]

        cr8, ci8 = [], []
        for k in range(tiles):
            apr = pre_ref[steps - 1:steps, cols[k]]
            api = pim_ref[steps - 1:steps, cols[k]]
            cr = hre[:, cols[k]]
            ci = him[:, cols[k]]
            crs, cis = [], []
            for s in range(SUBLANES):
                crs.append(cr)
                cis.append(ci)
                cr, ci = (apr * cr - api * ci + hr[k][s:s + 1, :], apr * ci + api * cr + hi[k][s:s + 1, :])
            hre[:, cols[k]] = cr
            him[:, cols[k]] = ci
            cr8.append(jnp.concatenate(crs, axis=0))
            ci8.append(jnp.concatenate(cis, axis=0))
        for r in range(steps):
            rows = slice(r * SUBLANES, (r + 1) * SUBLANES)
            for k, t in enumerate(ts):
                pr = jnp.broadcast_to(pre_ref[r:r + 1, cols[k]], (SUBLANES, LANES))
                pi = jnp.broadcast_to(pim_ref[r:r + 1, cols[k]], (SUBLANES, LANES))
                sre[t, rows, :] = sre[t, rows, :] + (pr * cr8[k] - pi * ci8[k])
                sim[t, rows, :] = sim[t, rows, :] + (pr * ci8[k] + pi * cr8[k])

        h_re = jnp.concatenate([sre[t] for t in ts], axis=1).astype(BF16)
        h_im = jnp.concatenate([sim[t] for t in ts], axis=1).astype(BF16)
        y = _dot(h_re, wc_ref[j, :sw, :]) + _dot(h_im, wc_ref[j, sw:, :])
        lc = slice(j * LANES, (j + 1) * LANES)
        y = y + d_ref[:, lc] * u[:, lc]
        y_scr[:, lc] = jax.nn.gelu(y)

    y = y_scr[...]
    gate = _dot(y.astype(BF16), gw_ref[...]) + gb_ref[...]
    out = (y * jax.nn.sigmoid(gate)).astype(BF16)
    o_ref[...] = _dot(unperm_ref[...], out).astype(o_ref.dtype)


def _s5_prepare(lam_re, lam_im, log_step, b_re, b_im, c_re, c_im, d, steps):
    g, p = lam_re.shape
    h = S5_CH_PER_GROUP
    q = S5_GROUPS_PER_SLAB
    n_slabs = g // q
    step = jnp.exp(log_step.astype(F32))[:, None]
    lr, li = lam_re.astype(F32), lam_im.astype(F32)
    mag = jnp.exp(lr * step)
    a_re, a_im = mag * jnp.cos(li * step), mag * jnp.sin(li * step)
    den = lr * lr + li * li
    f_re = ((a_re - 1.0) * lr + a_im * li) / den
    f_im = (a_im * lr - (a_re - 1.0) * li) / den
    bb_re = f_re[:, :, None] * b_re - f_im[:, :, None] * b_im
    bb_im = f_re[:, :, None] * b_im + f_im[:, :, None] * b_re
    eye = jnp.eye(q, dtype=F32)

    def in_block(m):
        m = m.reshape(n_slabs, q, p, h)
        return jnp.einsum('jqph,qr->jqhrp', m, eye).reshape(n_slabs, q * h, q * p)

    def out_block(m):
        m = m.reshape(n_slabs, q, h, p)
        return jnp.einsum('jqhp,qr->jqprh', m, eye).reshape(n_slabs, q * p, q * h)

    wb = jnp.concatenate([in_block(bb_re), in_block(bb_im)], axis=2).astype(BF16)
    wc = jnp.concatenate([out_block(c_re.astype(F32)), out_block(-c_im.astype(F32))], axis=1).astype(BF16)
    k = jnp.arange(1, steps + 1, dtype=F32)[:, None, None]
    pmag = jnp.exp(lr * step * k)
    ang = li * step * k
    pw_re = (pmag * jnp.cos(ang)).reshape(steps, g * p)
    pw_im = (pmag * jnp.sin(ang)).reshape(steps, g * p)
    return (wb, wc, a_re.reshape(1, g * p), a_im.reshape(1, g * p), pw_re, pw_im,
            d.astype(F32).reshape(1, g * h))


def _s5(proj, bsz, seq, prep, glu_w, glu_b, tl=256):
    wb, wc, a_re, a_im, pw_re, pw_im, d = prep
    tl = min(tl, seq)
    steps = tl // SUBLANES
    width = d.shape[1]
    states = pw_re.shape[1]
    nl = seq // tl
    full = lambda a: pl.BlockSpec(a.shape, lambda b, l: (0,) * a.ndim)
    gw = glu_w.astype(BF16)
    gb = glu_b.astype(F32).reshape(1, width)
    dest = jnp.arange(tl)
    src = (dest % SUBLANES) * steps + dest // SUBLANES
    perm = (src[:, None] == jnp.arange(tl)[None, :]).astype(BF16)
    unperm = perm.T
    return pl.pallas_call(
        functools.partial(_s5_body, tl=tl, steps=steps),
        grid=(bsz, nl),
        in_specs=[pl.BlockSpec((tl, width), lambda b, l: (b * nl + l, 0)), full(perm), full(unperm),
                  full(wb), full(wc), full(a_re), full(a_im), full(pw_re), full(pw_im), full(d), full(gw), full(gb)],
        out_specs=pl.BlockSpec((tl, width), lambda b, l: (b * nl + l, 0)),
        out_shape=jax.ShapeDtypeStruct((bsz * seq, width), BF16),
        scratch_shapes=[pltpu.VMEM((states // LANES, tl, LANES), F32), pltpu.VMEM((states // LANES, tl, LANES), F32),
                        pltpu.VMEM((1, states), F32), pltpu.VMEM((1, states), F32),
                        pltpu.VMEM((tl, width), F32)],
        compiler_params=_params(("arbitrary", "arbitrary"), 40),
        name="s5",
    )(proj, perm, unperm, wb, wc, a_re, a_im, pw_re, pw_im, d, gw, gb)


def _sgu_body(zu_ref, zv_ref, g_ref, b_ref, w_ref, bias_ref, o_ref, *, tl):
    width = zu_ref.shape[1]
    u = jax.nn.gelu(zu_ref[...])
    v = _layer_norm(jax.nn.gelu(zv_ref[...]), g_ref[...], b_ref[...])
    vb = v.astype(BF16)
    row = lax.broadcasted_iota(jnp.int32, (2 * SGU_CHUNK, SGU_CHUNK), 0)
    col = lax.broadcasted_iota(jnp.int32, (2 * SGU_CHUNK, SGU_CHUNK), 1)
    causal = col <= (row % SGU_CHUNK)
    lane = lax.broadcasted_iota(jnp.int32, (SGU_CHUNK, LANES), 1)
    first = lane < SGU_HEAD_DIM
    for j in range(width // LANES):
        w = jnp.where(causal, w_ref[j], 0.0).astype(BF16)
        bias = bias_ref[j]
        for c in range(tl // SGU_CHUNK):
            rs = slice(c * SGU_CHUNK, (c + 1) * SGU_CHUNK)
            cs = slice(j * LANES, (j + 1) * LANES)
            r = _dot(w, vb[rs, cs])
            mixed = jnp.where(first, r[:SGU_CHUNK], r[SGU_CHUNK:]) + bias
            o_ref[rs, cs] = (u[rs, cs] * mixed).astype(o_ref.dtype)


def _sgu(proj, n, width, norm_g, norm_b, w_s, b_s, tl=512):
    heads = w_s.shape[0]
    tl = min(tl, n)
    w_pairs = w_s.astype(F32).reshape(heads // 2, 2 * SGU_CHUNK, SGU_CHUNK)
    bias = jnp.repeat(b_s.astype(F32).T, SGU_HEAD_DIM, axis=1)
    bias = bias.reshape(SGU_CHUNK, heads // 2, LANES).transpose(1, 0, 2)
    g = norm_g.astype(F32).reshape(1, width)
    b = norm_b.astype(F32).reshape(1, width)
    full = lambda a: pl.BlockSpec(a.shape, lambda i: (0,) * a.ndim)
    return pl.pallas_call(
        functools.partial(_sgu_body, tl=tl),
        grid=(n // tl,),
        in_specs=[pl.BlockSpec((tl, width), lambda i: (i, 1)),
                  pl.BlockSpec((tl, width), lambda i: (i, 2)),
                  full(g), full(b), full(w_pairs), full(bias)],
        out_specs=pl.BlockSpec((tl, width), lambda i: (i, 0)),
        out_shape=jax.ShapeDtypeStruct((n, width), BF16),
        compiler_params=_params(("parallel",), 40),
        name="sgu",
    )(proj, proj, g, b, w_pairs, bias)


def _pool_body(p_ref, w_ref, sc_ref, o_ref, ext, *, tl):
    l = pl.program_id(1)

    @pl.when(l == 0)
    def _():
        ext[0:POOL_HALO, :] = jnp.zeros((POOL_HALO, ext.shape[1]), F32)

    @pl.when(l > 0)
    def _():
        ext[0:POOL_HALO, :] = ext[tl:tl + POOL_HALO, :]

    ext[POOL_HALO:POOL_HALO + tl, :] = p_ref[...]
    pos = l * tl + lax.broadcasted_iota(jnp.int32, (tl, LANES), 0)
    for gi, win in enumerate(POOL_WINDOWS):
        cs = slice(gi * LANES, (gi + 1) * LANES)
        x = ext[POOL_HALO:POOL_HALO + tl, cs]
        acc = x
        for j in range(1, win):
            acc = acc + ext[POOL_HALO - j:POOL_HALO - j + tl, cs]
        count = jnp.minimum(pos + 1, win).astype(F32)
        pooled = acc / count - x
        y = _dot(pooled.astype(BF16), w_ref[gi]) * sc_ref[:, cs]
        o_ref[:, cs] = y.astype(o_ref.dtype)


def _pool(proj, bsz, seq, width, w_pool, scale, col_block, tl=512):
    tl = min(tl, seq)
    nl = seq // tl
    w = w_pool.astype(BF16)
    sc = scale.astype(F32).reshape(1, width)
    return pl.pallas_call(
        functools.partial(_pool_body, tl=tl),
        grid=(bsz, nl),
        in_specs=[pl.BlockSpec((tl, width), lambda b, l: (b * nl + l, col_block)),
                  pl.BlockSpec(w.shape, lambda b, l: (0, 0, 0)),
                  pl.BlockSpec(sc.shape, lambda b, l: (0, 0))],
        out_specs=pl.BlockSpec((tl, width), lambda b, l: (b * nl + l, 0)),
        out_shape=jax.ShapeDtypeStruct((bsz * seq, width), BF16),
        scratch_shapes=[pltpu.VMEM((tl + POOL_HALO, width), F32)],
        compiler_params=_params(("arbitrary", "arbitrary"), 40),
        name="pool",
    )(proj, w, sc)


def _dn_prep_body(q_ref, k_ref, v_ref, tail_ref, cw_ref, alog_ref, dtb_ref,
                  vb_ref, kbg_ref, qg_ref, kt_ref, qk_ref, low_ref, gl_ref, ext, *, tl, heads):
    l = pl.program_id(1)
    width = heads * DN_HEAD_DIM
    c = DN_CHUNK

    @pl.when(l == 0)
    def _():
        ext[:, 0:DN_HALO, :] = jnp.zeros((3, DN_HALO, width), F32)

    @pl.when(l > 0)
    def _():
        ext[:, 0:DN_HALO, :] = ext[:, tl:tl + DN_HALO, :]

    mixed = []
    for i, ref in enumerate((q_ref, k_ref, v_ref)):
        ext[i, DN_HALO:DN_HALO + tl, :] = ref[...]
        acc = jnp.zeros((tl, width), F32)
        for j in range(DN_CONV):
            off = DN_HALO - (DN_CONV - 1) + j
            acc = acc + ext[i, off:off + tl, :] * cw_ref[j:j + 1, i * width:(i + 1) * width]
        mixed.append(acc * jax.nn.sigmoid(acc))
    qc, kc, vc = mixed

    tail = tail_ref[...]
    sp_in = tail + dtb_ref[...]
    softplus = jnp.maximum(sp_in, 0.0) + jnp.log1p(jnp.exp(-jnp.abs(sp_in)))
    g_all = -jnp.exp(alog_ref[...]) * softplus
    beta_all = jax.nn.sigmoid(tail)

    ri = lax.broadcasted_iota(jnp.int32, (c, c), 0)
    ci = lax.broadcasted_iota(jnp.int32, (c, c), 1)
    causal = ri >= ci
    strict = ri > ci
    tri = causal.astype(F32)
    sel_lane = lax.broadcasted_iota(jnp.int32, (c, LANES), 1)

    for n in range(tl // c):
        rs = slice(n * c, (n + 1) * c)
        gc_all = jnp.dot(tri, g_all[rs, :], precision=HIGHEST, preferred_element_type=F32)
        for h in range(heads):
            hs = slice(h * DN_HEAD_DIM, (h + 1) * DN_HEAD_DIM)
            ms = slice(h * c, (h + 1) * c)
            q = qc[rs, hs]
            k = kc[rs, hs]
            q = q * lax.rsqrt(jnp.sum(q * q, axis=-1, keepdims=True) + L2_EPS) * (DN_HEAD_DIM ** -0.5)
            k = k * lax.rsqrt(jnp.sum(k * k, axis=-1, keepdims=True) + L2_EPS)
            beta = beta_all[rs, heads + h:heads + h + 1]
            gcol = gc_all[:, h:h + 1]
            glast = gc_all[c - 1:c, h:h + 1]
            sel = (sel_lane == h).astype(F32)
            grow = _dot_nt(sel, gc_all, precision=HIGHEST)
            decay = jnp.exp(jnp.where(causal, gcol - grow, -jnp.inf))
            kb = k * beta
            kbf = k.astype(BF16)
            egc = jnp.exp(gcol)
            low_ref[rs, ms] = jnp.where(strict, _dot_nt(kb.astype(BF16), kbf) * decay, 0.0)
            qk_ref[rs, ms] = _dot_nt(q.astype(BF16), kbf) * decay
            vb_ref[rs, hs] = (vc[rs, hs] * beta).astype(BF16)
            kbg_ref[rs, hs] = (kb * egc).astype(BF16)
            qg_ref[rs, hs] = (q * egc).astype(BF16)
            kt_ref[rs, hs] = (k * jnp.exp(glast - gcol)).astype(BF16)
            gl_ref[n * heads + h:n * heads + h + 1, :] = jnp.broadcast_to(jnp.exp(glast), (1, LANES))


def _tri_inv_body(l_ref, t_ref):
    c = l_ref.shape[0]
    groups = c // SUBLANES
    sub = lax.broadcasted_iota(jnp.int32, (SUBLANES, LANES), 0)
    zero = jnp.zeros((SUBLANES, LANES), F32)
    for i in range(c):
        live = i // SUBLANES + 1
        acc = [zero] * live
        acc[live - 1] = (sub == i % SUBLANES).astype(F32)
        for j in range(i):
            lij = jnp.broadcast_to(l_ref[i, j:j + 1, :], (SUBLANES, LANES))
            for cg in range(j // SUBLANES + 1):
                acc[cg] = acc[cg] - lij * t_ref[j, cg * SUBLANES:(cg + 1) * SUBLANES, :]
        for cg in range(groups):
            t_ref[i, cg * SUBLANES:(cg + 1) * SUBLANES, :] = acc[cg] if cg < live else zero


def _dn_rec_body(t_ref, vb_ref, kbg_ref, qg_ref, kt_ref, qk_ref, gl_ref, gate_ref, ng_ref, o_ref, state,
                 *, tl, heads):
    c = DN_CHUNK

    @pl.when(pl.program_id(1) == 0)
    def _():
        state[...] = jnp.zeros_like(state)

    for n in range(tl // c):
        rs = slice(n * c, (n + 1) * c)
        for h in range(heads):
            hs = slice(h * DN_HEAD_DIM, (h + 1) * DN_HEAD_DIM)
            ms = slice(h * c, (h + 1) * c)
            t_mat = t_ref[rs, ms].astype(BF16)
            u = _dot(t_mat, vb_ref[rs, hs])
            w = _dot(t_mat, kbg_ref[rs, hs])
            s = state[h]
            sb = s.astype(BF16)
            v_new = u - _dot(w.astype(BF16), sb)
            vnb = v_new.astype(BF16)
            o = _dot(qg_ref[rs, hs], sb) + _dot(qk_ref[rs, ms].astype(BF16), vnb)
            state[h] = s * gl_ref[n * heads + h:n * heads + h + 1, :] + _dot_tn(kt_ref[rs, hs], vnb)
            o = o * lax.rsqrt(jnp.mean(o * o, axis=-1, keepdims=True) + RMS_EPS) * ng_ref[...]
            gt = gate_ref[rs, hs]
            o_ref[rs, hs] = (o * (gt * jax.nn.sigmoid(gt))).astype(o_ref.dtype)


def _dn(proj, tail, bsz, seq, width, col_block, conv_w, a_log, dt_bias, norm_g, tl=256):
    heads = width // DN_HEAD_DIM
    c = DN_CHUNK
    n = bsz * seq
    tl = min(tl, seq)
    nl = seq // tl
    chunks = n // c
    cw = conv_w.astype(F32)
    pad = lambda a: jnp.pad(a.astype(F32), (0, LANES - heads)).reshape(1, LANES)
    alog = pad(a_log)
    dtb = pad(dt_bias)
    ng = norm_g.astype(F32).reshape(1, DN_HEAD_DIM)
    full = lambda a: pl.BlockSpec(a.shape, lambda b, l: (0,) * a.ndim)
    slab = lambda k: pl.BlockSpec((tl, width), lambda b, l: (b * nl + l, col_block + k))
    rows = lambda wd: pl.BlockSpec((tl, wd), lambda b, l: (b * nl + l, 0))
    gl_spec = pl.BlockSpec((tl // c * heads, LANES), lambda b, l: (b * nl + l, 0))
    wide = jax.ShapeDtypeStruct((n, width), BF16)
    mats = jax.ShapeDtypeStruct((n, heads * c), F32)
    vb, kbg, qg, kt, qk, low, gl = pl.pallas_call(
        functools.partial(_dn_prep_body, tl=tl, heads=heads),
        grid=(bsz, nl),
        in_specs=[slab(0), slab(1), slab(2), rows(LANES), full(cw), full(alog), full(dtb)],
        out_specs=[rows(width)] * 4 + [rows(heads * c)] * 2 + [gl_spec],
        out_shape=[wide] * 4 + [mats] * 2 + [jax.ShapeDtypeStruct((chunks * heads, LANES), F32)],
        scratch_shapes=[pltpu.VMEM((3, tl + DN_HALO, width), F32)],
        compiler_params=_params(("arbitrary", "arbitrary"), 40),
        name="deltanet_prep",
    )(proj, proj, proj, tail, cw, alog, dtb)

    m_total = chunks * heads
    low_t = low.reshape(chunks, c, heads, c).transpose(1, 3, 0, 2).reshape(c, c, m_total)
    t_t = pl.pallas_call(
        _tri_inv_body,
        grid=(m_total // LANES,),
        in_specs=[pl.BlockSpec((c, c, LANES), lambda i: (0, 0, i))],
        out_specs=pl.BlockSpec((c, c, LANES), lambda i: (0, 0, i)),
        out_shape=jax.ShapeDtypeStruct((c, c, m_total), F32),
        compiler_params=_params(("parallel",), 40),
        name="deltanet_tri_inv",
    )(low_t)
    t_mat = t_t.reshape(c, c, chunks, heads).transpose(2, 0, 3, 1).reshape(n, heads * c)

    return pl.pallas_call(
        functools.partial(_dn_rec_body, tl=tl, heads=heads),
        grid=(bsz, nl),
        in_specs=[rows(heads * c)] + [rows(width)] * 4 + [rows(heads * c), gl_spec, slab(3), full(ng)],
        out_specs=rows(width),
        out_shape=wide,
        scratch_shapes=[pltpu.VMEM((heads, DN_HEAD_DIM, DN_HEAD_DIM), F32)],
        compiler_params=_params(("arbitrary", "arbitrary"), 40),
        name="deltanet_rec",
    )(t_mat, vb, kbg, qg, kt, qk, gl, proj, ng)


def _out_body(y0_ref, y1_ref, y2_ref, y3_ref, w_ref, x_ref, g_ref, b_ref, o_ref, ob_ref, *, alpha):
    width = y0_ref.shape[1]
    acc = alpha * x_ref[...]
    for i, ref in enumerate((y0_ref, y1_ref, y2_ref, y3_ref)):
        acc = acc + _dot(ref[...], w_ref[i * width:(i + 1) * width, :])
    y = _layer_norm(acc, g_ref[...], b_ref[...])
    o_ref[...] = y
    ob_ref[...] = y.astype(BF16)


def _out_proj(ys, w_out, x, g, b, alpha, tm=512):
    n, d = x.shape
    width = ys[0].shape[1]
    tm = min(tm, n)
    row = lambda wd: pl.BlockSpec((tm, wd), lambda i: (i, 0))
    full = lambda a: pl.BlockSpec(a.shape, lambda i: (0,) * a.ndim)
    g = g.astype(F32).reshape(1, d)
    b = b.astype(F32).reshape(1, d)
    return pl.pallas_call(
        functools.partial(_out_body, alpha=alpha),
        grid=(n // tm,),
        in_specs=[row(width)] * 4 + [full(w_out), row(d), full(g), full(b)],
        out_specs=[row(d), row(d)],
        out_shape=[jax.ShapeDtypeStruct((n, d), F32), jax.ShapeDtypeStruct((n, d), BF16)],
        compiler_params=_params(("parallel",), 56),
        name="out_proj_ln",
    )(*ys, w_out, x, g, b)


def _ffn_body(xb_ref, wu_ref, wd_ref, x_ref, g_ref, b_ref, o_ref, ob_ref, acc, *, alpha):
    f = pl.program_id(1)

    @pl.when(f == 0)
    def _():
        acc[...] = alpha * x_ref[...]

    hidden = jnp.maximum(_dot(xb_ref[...], wu_ref[...]), 0.0)
    acc[...] += _dot((hidden * hidden).astype(BF16), wd_ref[...])

    @pl.when(f == pl.num_programs(1) - 1)
    def _():
        y = _layer_norm(acc[...], g_ref[...], b_ref[...])
        o_ref[...] = y
        ob_ref[...] = y.astype(BF16)


def _ffn(xb, x, w_up, w_down, g, b, alpha, tm=512, tf=512):
    n, d = x.shape
    ff = w_up.shape[1]
    tm = min(tm, n)
    g = g.astype(F32).reshape(1, d)
    b = b.astype(F32).reshape(1, d)
    row = pl.BlockSpec((tm, d), lambda i, f: (i, 0))
    vec = pl.BlockSpec((1, d), lambda i, f: (0, 0))
    return pl.pallas_call(
        functools.partial(_ffn_body, alpha=alpha),
        grid=(n // tm, ff // tf),
        in_specs=[row, pl.BlockSpec((d, tf), lambda i, f: (0, f)), pl.BlockSpec((tf, d), lambda i, f: (f, 0)),
                  row, vec, vec],
        out_specs=[row, row],
        out_shape=[jax.ShapeDtypeStruct((n, d), F32), jax.ShapeDtypeStruct((n, d), BF16)],
        scratch_shapes=[pltpu.VMEM((tm, d), F32)],
        compiler_params=_params(("parallel", "arbitrary"), 56),
        name="ffn_ln",
    )(xb, w_up, w_down, x, g, b)


def kernel(x, w_in, s5_lambda_re, s5_lambda_im, s5_log_step, s5_b_re, s5_b_im, s5_c_re, s5_c_im, s5_d, s5_glu_w, s5_glu_b, sgu_norm_g, sgu_norm_b, sgu_w, sgu_b, pool_w, pool_scale, dn_conv_w, dn_a_log, dn_dt_bias, dn_norm_g, w_out, ln1_g, ln1_b, w_up, w_down, ln2_g, ln2_b):
    bsz, seq, d = x.shape
    depth = w_in.shape[0]
    n = bsz * seq
    width = s5_glu_w.shape[1]
    heads = dn_a_log.shape[1]
    main_cols = 8 * width
    alpha = (2 * depth) ** 0.25
    s5_tl = min(256, seq)

    xf = x.reshape(n, d).astype(F32)
    xb = xf.astype(BF16)
    for i in range(depth):
        w_main = w_in[i, :, :main_cols].astype(BF16)
        w_tail = jnp.pad(w_in[i, :, main_cols:], ((0, 0), (0, LANES - 2 * heads))).astype(BF16)
        proj, tail = _proj(xb, w_main, w_tail)
        prep = _s5_prepare(s5_lambda_re[i], s5_lambda_im[i], s5_log_step[i], s5_b_re[i], s5_b_im[i],
                           s5_c_re[i], s5_c_im[i], s5_d[i], s5_tl // SUBLANES)
        y_s5 = _s5(proj, bsz, seq, prep, s5_glu_w[i], s5_glu_b[i], tl=s5_tl)
        y_sgu = _sgu(proj, n, width, sgu_norm_g[i], sgu_norm_b[i], sgu_w[i], sgu_b[i])
        y_pool = _pool(proj, bsz, seq, width, pool_w[i], pool_scale[i], col_block=3)
        y_dn = _dn(proj, tail, bsz, seq, width, 4, dn_conv_w[i], dn_a_log[i], dn_dt_bias[i], dn_norm_g[i])
        xf, xb = _out_proj((y_s5, y_sgu, y_pool, y_dn), w_out[i].astype(BF16), xf, ln1_g[i], ln1_b[i], alpha)
        xf, xb = _ffn(xb, xf, w_up[i].astype(BF16), w_down[i].astype(BF16), ln2_g[i], ln2_b[i], alpha)
    return xf.reshape(bsz, seq, d).astype(x.dtype)
```

```python
import functools
import math

import jax
import jax.numpy as jnp
from jax import lax
from jax.experimental import pallas as pl
from jax.experimental.pallas import tpu as pltpu

F32 = jnp.float32
BF16 = jnp.bfloat16
HIGHEST = lax.Precision.HIGHEST

LANES = 128
SUBLANES = 8

S5_CH_PER_GROUP = 16
S5_STATE = 64
S5_GROUPS_PER_SLAB = LANES // S5_CH_PER_GROUP
S5_SLAB_STATE = S5_GROUPS_PER_SLAB * S5_STATE
SGU_CHUNK = 128
SGU_HEAD_DIM = 64
POOL_WINDOWS = (2, 4, 8, 16)
POOL_HALO = 16
DN_HEAD_DIM = 128
DN_CONV = 4
DN_CHUNK = 64
DN_HALO = 8
LN_EPS = 1e-5
RMS_EPS = 1e-6
L2_EPS = 1e-6


def _params(semantics, vmem_mib):
    return pltpu.CompilerParams(dimension_semantics=semantics, vmem_limit_bytes=vmem_mib * 1024 * 1024)


def _layer_norm(y, g, b):
    mu = jnp.mean(y, axis=-1, keepdims=True)
    yc = y - mu
    var = jnp.mean(yc * yc, axis=-1, keepdims=True)
    return yc * lax.rsqrt(var + LN_EPS) * g + b


def _dot(a, b):
    return jnp.dot(a, b, preferred_element_type=F32)


def _dot_nt(a, b, precision=None):
    return lax.dot_general(a, b, (((1,), (1,)), ((), ())), precision=precision, preferred_element_type=F32)


def _dot_tn(a, b):
    return lax.dot_general(a, b, (((0,), (0,)), ((), ())), preferred_element_type=F32)


def _split3(x):
    hi = x.astype(BF16)
    rest = x - hi.astype(F32)
    mid = rest.astype(BF16)
    lo = (rest - mid.astype(F32)).astype(BF16)
    return hi, mid, lo


def _proj_body(x_ref, w_ref, wt_ref, o_ref, ot_ref):
    xb = x_ref[...]
    o_ref[...] = _dot(xb, w_ref[...])

    @pl.when(pl.program_id(1) == 0)
    def _():
        ot_ref[...] = _dot(xb, wt_ref[...])


def _proj(xb, w_main, w_tail, tm=1024, tn=512):
    n, d = xb.shape
    cols = w_main.shape[1]
    tm = min(tm, n)
    return pl.pallas_call(
        _proj_body,
        grid=(n // tm, cols // tn),
        in_specs=[pl.BlockSpec((tm, d), lambda i, j: (i, 0)),
                  pl.BlockSpec((d, tn), lambda i, j: (0, j)),
                  pl.BlockSpec((d, LANES), lambda i, j: (0, 0))],
        out_specs=[pl.BlockSpec((tm, tn), lambda i, j: (i, j)),
                   pl.BlockSpec((tm, LANES), lambda i, j: (i, 0))],
        out_shape=[jax.ShapeDtypeStruct((n, cols), F32), jax.ShapeDtypeStruct((n, LANES), F32)],
        compiler_params=_params(("parallel", "arbitrary"), 40),
        name="proj",
    )(xb, w_main, w_tail)


def _s5_body(u_ref, perm_ref, unperm_ref, wb_ref, wc_ref, are_ref, aim_ref, pre_ref, pim_ref, d_ref, gw_ref, gb_ref,
             o_ref, sre, sim, hre, him, y_scr, *, tl, steps):
    n_slabs = u_ref.shape[1] // LANES
    sw = S5_SLAB_STATE
    tiles = sw // LANES

    @pl.when(pl.program_id(1) == 0)
    def _():
        hre[...] = jnp.zeros_like(hre)
        him[...] = jnp.zeros_like(him)

    u_nat = u_ref[...]
    perm = perm_ref[...]
    u1, u2, u3 = _split3(u_nat)
    u = _dot(perm, u1) + _dot(perm, u2) + _dot(perm, u3)
    ub = u.astype(BF16)

    for j in range(n_slabs):
        bu = _dot(ub[:, j * LANES:(j + 1) * LANES], wb_ref[j])
        for k in range(tiles):
            sre[j * tiles + k] = bu[:, k * LANES:(k + 1) * LANES]
            sim[j * tiles + k] = bu[:, sw + k * LANES:sw + (k + 1) * LANES]

    group = tiles
    for j in range(n_slabs):
        ts = [j * tiles + k for k in range(tiles)]
        cols = [slice(t * LANES, (t + 1) * LANES) for t in ts]

        ar = [jnp.broadcast_to(are_ref[:, c], (SUBLANES, LANES)) for c in cols]
        ai = [jnp.broadcast_to(aim_ref[:, c], (SUBLANES, LANES)) for c in cols]
        hr = [jnp.zeros((SUBLANES, LANES), F32)] * group
        hi = [jnp.zeros((SUBLANES, LANES), F32)] * group
        for r in range(steps):
            rows = slice(r * SUBLANES, (r + 1) * SUBLANES)
            for k, t in enumerate(ts):
                hr[k], hi[k] = (ar[k] * hr[k] - ai[k] * hi[k] + sre[t, rows, :],
                                ar[k] * hi[k] + ai[k] * hr[k] + sim[t, rows, :])
                sre[t, rows, :] = hr[k]
                sim[t, rows, :] = hi[k]

        cr8, ci8 = [], []
        for k in range(group):
            apr = pre_ref[steps - 1:steps, cols[k]]
            api = pim_ref[steps - 1:steps, cols[k]]
            cr = hre[:, cols[k]]
            ci = him[:, cols[k]]
            crs, cis = [], []
            for s in range(SUBLANES):
                crs.append(cr)
                cis.append(ci)
                cr, ci = (apr * cr - api * ci + hr[k][s:s + 1, :], apr * ci + api * cr + hi[k][s:s + 1, :])
            hre[:, cols[k]] = cr
            him[:, cols[k]] = ci
            cr8.append(jnp.concatenate(crs, axis=0))
            ci8.append(jnp.concatenate(cis, axis=0))
        for r in range(steps):
            rows = slice(r * SUBLANES, (r + 1) * SUBLANES)
            for k, t in enumerate(ts):
                pr = jnp.broadcast_to(pre_ref[r:r + 1, cols[k]], (SUBLANES, LANES))
                pi = jnp.broadcast_to(pim_ref[r:r + 1, cols[k]], (SUBLANES, LANES))
                sre[t, rows, :] = sre[t, rows, :] + (pr * cr8[k] - pi * ci8[k])
                sim[t, rows, :] = sim[t, rows, :] + (pr * ci8[k] + pi * cr8[k])

        h_re = jnp.concatenate([sre[t] for t in ts], axis=1).astype(BF16)
        h_im = jnp.concatenate([sim[t] for t in ts], axis=1).astype(BF16)
        y = _dot(h_re, wc_ref[j, :sw, :]) + _dot(h_im, wc_ref[j, sw:, :])
        lc = slice(j * LANES, (j + 1) * LANES)
        y = y + d_ref[:, lc] * u[:, lc]
        y_scr[:, lc] = jax.nn.gelu(y)

    y = y_scr[...]
    gate = _dot(y.astype(BF16), gw_ref[...]) + gb_ref[...]
    out = (y * jax.nn.sigmoid(gate)).astype(BF16)
    o_ref[...] = _dot(unperm_ref[...], out).astype(o_ref.dtype)


def _s5_prepare(lam_re, lam_im, log_step, b_re, b_im, c_re, c_im, d, steps):
    g, p = lam_re.shape
    h = S5_CH_PER_GROUP
    q = S5_GROUPS_PER_SLAB
    n_slabs = g // q
    step = jnp.exp(log_step.astype(F32))[:, None]
    lr, li = lam_re.astype(F32), lam_im.astype(F32)
    mag = jnp.exp(lr * step)
    a_re, a_im = mag * jnp.cos(li * step), mag * jnp.sin(li * step)
    den = lr * lr + li * li
    f_re = ((a_re - 1.0) * lr + a_im * li) / den
    f_im = (a_im * lr - (a_re - 1.0) * li) / den
    bb_re = f_re[:, :, None] * b_re - f_im[:, :, None] * b_im
    bb_im = f_re[:, :, None] * b_im + f_im[:, :, None] * b_re
    eye = jnp.eye(q, dtype=F32)

    def in_block(m):
        m = m.reshape(n_slabs, q, p, h)
        return jnp.einsum('jqph,qr->jqhrp', m, eye).reshape(n_slabs, q * h, q * p)

    def out_block(m):
        m = m.reshape(n_slabs, q, h, p)
        return jnp.einsum('jqhp,qr->jqprh', m, eye).reshape(n_slabs, q * p, q * h)

    wb = jnp.concatenate([in_block(bb_re), in_block(bb_im)], axis=2).astype(BF16)
    wc = jnp.concatenate([out_block(c_re.astype(F32)), out_block(-c_im.astype(F32))], axis=1).astype(BF16)
    k = jnp.arange(1, steps + 1, dtype=F32)[:, None, None]
    pmag = jnp.exp(lr * step * k)
    ang = li * step * k
    pw_re = (pmag * jnp.cos(ang)).reshape(steps, g * p)
    pw_im = (pmag * jnp.sin(ang)).reshape(steps, g * p)
    return (wb, wc, a_re.reshape(1, g * p), a_im.reshape(1, g * p), pw_re, pw_im,
            d.astype(F32).reshape(1, g * h))


def _s5(proj, bsz, seq, prep, glu_w, glu_b, tl=256):
    wb, wc, a_re, a_im, pw_re, pw_im, d = prep
    tl = min(tl, seq)
    steps = tl // SUBLANES
    width = d.shape[1]
    states = pw_re.shape[1]
    nl = seq // tl
    full = lambda a: pl.BlockSpec(a.shape, lambda b, l: (0,) * a.ndim)
    gw = glu_w.astype(BF16)
    gb = glu_b.astype(F32).reshape(1, width)
    dest = jnp.arange(tl)
    src = (dest % SUBLANES) * steps + dest // SUBLANES
    perm = (src[:, None] == jnp.arange(tl)[None, :]).astype(BF16)
    unperm = perm.T
    return pl.pallas_call(
        functools.partial(_s5_body, tl=tl, steps=steps),
        grid=(bsz, nl),
        in_specs=[pl.BlockSpec((tl, width), lambda b, l: (b * nl + l, 0)), full(perm), full(unperm),
                  full(wb), full(wc), full(a_re), full(a_im), full(pw_re), full(pw_im), full(d), full(gw), full(gb)],
        out_specs=pl.BlockSpec((tl, width), lambda b, l: (b * nl + l, 0)),
        out_shape=jax.ShapeDtypeStruct((bsz * seq, width), BF16),
        scratch_shapes=[pltpu.VMEM((states // LANES, tl, LANES), F32), pltpu.VMEM((states // LANES, tl, LANES), F32),
                        pltpu.VMEM((1, states), F32), pltpu.VMEM((1, states), F32),
                        pltpu.VMEM((tl, width), F32)],
        compiler_params=_params(("arbitrary", "arbitrary"), 40),
        name="s5",
    )(proj, perm, unperm, wb, wc, a_re, a_im, pw_re, pw_im, d, gw, gb)


def _sgu_body(zu_ref, zv_ref, g_ref, b_ref, w_ref, bias_ref, o_ref, *, tl):
    width = zu_ref.shape[1]
    u = jax.nn.gelu(zu_ref[...])
    v = _layer_norm(jax.nn.gelu(zv_ref[...]), g_ref[...], b_ref[...])
    vb = v.astype(BF16)
    row = lax.broadcasted_iota(jnp.int32, (2 * SGU_CHUNK, SGU_CHUNK), 0)
    col = lax.broadcasted_iota(jnp.int32, (2 * SGU_CHUNK, SGU_CHUNK), 1)
    causal = col <= (row % SGU_CHUNK)
    lane = lax.broadcasted_iota(jnp.int32, (SGU_CHUNK, LANES), 1)
    first = lane < SGU_HEAD_DIM
    for j in range(width // LANES):
        w = jnp.where(causal, w_ref[j], 0.0).astype(BF16)
        bias = bias_ref[j]
        for c in range(tl // SGU_CHUNK):
            rs = slice(c * SGU_CHUNK, (c + 1) * SGU_CHUNK)
            cs = slice(j * LANES, (j + 1) * LANES)
            r = _dot(w, vb[rs, cs])
            mixed = jnp.where(first, r[:SGU_CHUNK], r[SGU_CHUNK:]) + bias
            o_ref[rs, cs] = (u[rs, cs] * mixed).astype(o_ref.dtype)


def _sgu(proj, n, width, norm_g, norm_b, w_s, b_s, tl=512):
    heads = w_s.shape[0]
    tl = min(tl, n)
    w_pairs = w_s.astype(F32).reshape(heads // 2, 2 * SGU_CHUNK, SGU_CHUNK)
    bias = jnp.repeat(b_s.astype(F32).T, SGU_HEAD_DIM, axis=1)
    bias = bias.reshape(SGU_CHUNK, heads // 2, LANES).transpose(1, 0, 2)
    g = norm_g.astype(F32).reshape(1, width)
    b = norm_b.astype(F32).reshape(1, width)
    full = lambda a: pl.BlockSpec(a.shape, lambda i: (0,) * a.ndim)
    return pl.pallas_call(
        functools.partial(_sgu_body, tl=tl),
        grid=(n // tl,),
        in_specs=[pl.BlockSpec((tl, width), lambda i: (i, 1)),
                  pl.BlockSpec((tl, width), lambda i: (i, 2)),
                  full(g), full(b), full(w_pairs), full(bias)],
        out_specs=pl.BlockSpec((tl, width), lambda i: (i, 0)),
        out_shape=jax.ShapeDtypeStruct((n, width), BF16),
        compiler_params=_params(("parallel",), 40),
        name="sgu",
    )(proj, proj, g, b, w_pairs, bias)


def _pool_body(p_ref, w_ref, sc_ref, o_ref, ext, *, tl):
    l = pl.program_id(1)

    @pl.when(l == 0)
    def _():
        ext[0:POOL_HALO, :] = jnp.zeros((POOL_HALO, ext.shape[1]), F32)

    @pl.when(l > 0)
    def _():
        ext[0:POOL_HALO, :] = ext[tl:tl + POOL_HALO, :]

    ext[POOL_HALO:POOL_HALO + tl, :] = p_ref[...]
    pos = l * tl + lax.broadcasted_iota(jnp.int32, (tl, LANES), 0)
    for gi, win in enumerate(POOL_WINDOWS):
        cs = slice(gi * LANES, (gi + 1) * LANES)
        x = ext[POOL_HALO:POOL_HALO + tl, cs]
        acc = x
        for j in range(1, win):
            acc = acc + ext[POOL_HALO - j:POOL_HALO - j + tl, cs]
        count = jnp.minimum(pos + 1, win).astype(F32)
        pooled = acc / count - x
        y = _dot(pooled.astype(BF16), w_ref[gi]) * sc_ref[:, cs]
        o_ref[:, cs] = y.astype(o_ref.dtype)


def _pool(proj, bsz, seq, width, w_pool, scale, col_block, tl=512):
    tl = min(tl, seq)
    nl = seq // tl
    w = w_pool.astype(BF16)
    sc = scale.astype(F32).reshape(1, width)
    return pl.pallas_call(
        functools.partial(_pool_body, tl=tl),
        grid=(bsz, nl),
        in_specs=[pl.BlockSpec((tl, width), lambda b, l: (b * nl + l, col_block)),
                  pl.BlockSpec(w.shape, lambda b, l: (0, 0, 0)),
                  pl.BlockSpec(sc.shape, lambda b, l: (0, 0))],
        out_specs=pl.BlockSpec((tl, width), lambda b, l: (b * nl + l, 0)),
        out_shape=jax.ShapeDtypeStruct((bsz * seq, width), BF16),
        scratch_shapes=[pltpu.VMEM((tl + POOL_HALO, width), F32)],
        compiler_params=_params(("arbitrary", "arbitrary"), 40),
        name="pool",
    )(proj, w, sc)


def _dn_prep_body(q_ref, k_ref, v_ref, tail_ref, cw_ref, alog_ref, dtb_ref,
                  vk_ref, qg_ref, kt_ref, qk_ref, low_ref, gl_ref, ext, *, tl, heads):
    l = pl.program_id(1)
    width = heads * DN_HEAD_DIM
    c = DN_CHUNK

    @pl.when(l == 0)
    def _():
        ext[:, 0:DN_HALO, :] = jnp.zeros((3, DN_HALO, width), F32)

    @pl.when(l > 0)
    def _():
        ext[:, 0:DN_HALO, :] = ext[:, tl:tl + DN_HALO, :]

    mixed = []
    for i, ref in enumerate((q_ref, k_ref, v_ref)):
        ext[i, DN_HALO:DN_HALO + tl, :] = ref[...]
        e = ext[i]
        acc = e * cw_ref[0:1, i * width:(i + 1) * width]
        for j in range(1, DN_CONV):
            acc = pltpu.roll(acc, 1, 0) + e * cw_ref[j:j + 1, i * width:(i + 1) * width]
        acc = acc[DN_HALO:, :]
        mixed.append(acc * jax.nn.sigmoid(acc))
    qc, kc, vc = mixed

    tail = tail_ref[...]
    sp_in = tail + dtb_ref[...]
    softplus = jnp.maximum(sp_in, 0.0) + jnp.log1p(jnp.exp(-jnp.abs(sp_in)))
    g_all = -jnp.exp(alog_ref[...]) * softplus
    beta_all = jax.nn.sigmoid(tail)

    ri = lax.broadcasted_iota(jnp.int32, (c, c), 0)
    ci = lax.broadcasted_iota(jnp.int32, (c, c), 1)
    causal = ri >= ci
    strict = ri > ci
    tri = causal.astype(BF16)
    sel = (lax.broadcasted_iota(jnp.int32, (heads * c, LANES), 0) // c
           == lax.broadcasted_iota(jnp.int32, (heads * c, LANES), 1)).astype(BF16)
    chunks = range(tl // c)

    gc = []
    for n in chunks:
        g1, g2, g3 = _split3(g_all[n * c:(n + 1) * c, :])
        gc.append(_dot(tri, g1) + _dot(tri, g2) + _dot(tri, g3))
    grow = []
    for n in chunks:
        g1, g2, g3 = _split3(gc[n])
        grow.append(_dot_nt(sel, g1) + _dot_nt(sel, g2) + _dot_nt(sel, g3))
    for n in chunks:
        rs = slice(n * c, (n + 1) * c)
        qn, kn, kbeta, kq = {}, {}, {}, {}
        for h in range(heads):
            hs = slice(h * DN_HEAD_DIM, (h + 1) * DN_HEAD_DIM)
            q = qc[rs, hs]
            k = kc[rs, hs]
            q = q * lax.rsqrt(jnp.sum(q * q, axis=-1, keepdims=True) + L2_EPS) * (DN_HEAD_DIM ** -0.5)
            k = k * lax.rsqrt(jnp.sum(k * k, axis=-1, keepdims=True) + L2_EPS)
            kb = k * beta_all[rs, heads + h:heads + h + 1]
            lhs = jnp.concatenate([kb.astype(BF16), q.astype(BF16)], axis=0)
            qn[n, h], kn[n, h], kbeta[n, h] = q, k, kb
            kq[n, h] = _dot_nt(lhs, k.astype(BF16))
        for h in range(heads):
            hs = slice(h * DN_HEAD_DIM, (h + 1) * DN_HEAD_DIM)
            ms = slice(h * c, (h + 1) * c)
            gcol = gc[n][:, h:h + 1]
            glast = gc[n][c - 1:c, h:h + 1]
            decay = jnp.exp(jnp.where(causal, gcol - grow[n][ms, :], -jnp.inf))
            egc = jnp.exp(gcol)
            low_ref[rs, ms] = jnp.where(strict, kq[n, h][:c] * decay, 0.0)
            qk_ref[rs, ms] = kq[n, h][c:] * decay
            beta = beta_all[rs, heads + h:heads + h + 1]
            vk_ref[rs, 2 * h * DN_HEAD_DIM:(2 * h + 1) * DN_HEAD_DIM] = (vc[rs, hs] * beta).astype(BF16)
            vk_ref[rs, (2 * h + 1) * DN_HEAD_DIM:(2 * h + 2) * DN_HEAD_DIM] = (kbeta[n, h] * egc).astype(BF16)
            qg_ref[rs, hs] = (qn[n, h] * egc).astype(BF16)
            kt_ref[rs, hs] = (kn[n, h] * jnp.exp(glast - gcol)).astype(BF16)
            gl_ref[n * heads + h:n * heads + h + 1, :] = jnp.broadcast_to(jnp.exp(glast), (1, LANES))


def _tri_inv_body(l_ref, t_ref):
    c = l_ref.shape[0]
    groups = c // SUBLANES
    sub = lax.broadcasted_iota(jnp.int32, (SUBLANES, LANES), 0)
    zero = jnp.zeros((SUBLANES, LANES), F32)
    for i in range(c):
        live = i // SUBLANES + 1
        acc = [zero] * live
        acc[live - 1] = (sub == i % SUBLANES).astype(F32)
        for j in range(i):
            lij = jnp.broadcast_to(l_ref[i, j:j + 1, :], (SUBLANES, LANES))
            for cg in range(j // SUBLANES + 1):
                acc[cg] = acc[cg] - lij * t_ref[j, cg * SUBLANES:(cg + 1) * SUBLANES, :]
        for cg in range(groups):
            t_ref[i, cg * SUBLANES:(cg + 1) * SUBLANES, :] = acc[cg] if cg < live else zero


def _dn_rec_body(t_ref, vk_ref, qg_ref, kt_ref, qk_ref, gl_ref, gate_ref, ng_ref, o_ref, state,
                 *, tl, heads):
    c = DN_CHUNK
    dk = DN_HEAD_DIM

    @pl.when(pl.program_id(1) == 0)
    def _():
        state[...] = jnp.zeros_like(state)

    chunks = range(tl // c)
    rows = lambda n: slice(n * c, (n + 1) * c)
    hcol = lambda h: slice(h * dk, (h + 1) * dk)
    mcol = lambda h: slice(h * c, (h + 1) * c)

    uw, uwb, kuw = {}, {}, {}
    for n in chunks:
        for h in range(heads):
            uw[n, h] = _dot(t_ref[rows(n), mcol(h)].astype(BF16), vk_ref[rows(n), 2 * h * dk:2 * (h + 1) * dk])
            uwb[n, h] = uw[n, h].astype(BF16)
    for n in chunks:
        for h in range(heads):
            kuw[n, h] = _dot_tn(kt_ref[rows(n), hcol(h)], uwb[n, h])

    def emit(n, h, wq, v_bf16):
        o = wq[c:] + _dot(qk_ref[rows(n), mcol(h)].astype(BF16), v_bf16)
        o = o * lax.rsqrt(jnp.mean(o * o, axis=-1, keepdims=True) + RMS_EPS) * ng_ref[...]
        gt = gate_ref[rows(n), hcol(h)]
        o_ref[rows(n), hcol(h)] = (o * (gt * jax.nn.sigmoid(gt))).astype(o_ref.dtype)

    s = [state[h] for h in range(heads)]
    pending = []
    for n in chunks:
        sb = [s[h].astype(BF16) for h in range(heads)]
        for h in range(heads):
            s[h] = (s[h] * gl_ref[n * heads + h:n * heads + h + 1, :] + kuw[n, h][:, :dk]
                    - _dot(kuw[n, h][:, dk:].astype(BF16), sb[h]))
        wq = []
        for h in range(heads):
            lhs = jnp.concatenate([uwb[n, h][:, dk:], qg_ref[rows(n), hcol(h)]], axis=0)
            wq.append(_dot(lhs, sb[h]))
        for args in pending:
            emit(*args)
        pending = [(n, h, wq[h], (uw[n, h][:, :dk] - wq[h][:c]).astype(BF16)) for h in range(heads)]
    for args in pending:
        emit(*args)
    for h in range(heads):
        state[h] = s[h]


def _dn(proj, tail, bsz, seq, width, col_block, conv_w, a_log, dt_bias, norm_g, tl=256):
    heads = width // DN_HEAD_DIM
    c = DN_CHUNK
    n = bsz * seq
    tl = min(tl, seq)
    nl = seq // tl
    chunks = n // c
    cw = conv_w.astype(F32)
    pad = lambda a: jnp.pad(a.astype(F32), (0, LANES - heads)).reshape(1, LANES)
    alog = pad(a_log)
    dtb = pad(dt_bias)
    ng = norm_g.astype(F32).reshape(1, DN_HEAD_DIM)
    full = lambda a: pl.BlockSpec(a.shape, lambda b, l: (0,) * a.ndim)
    slab = lambda k: pl.BlockSpec((tl, width), lambda b, l: (b * nl + l, col_block + k))
    rows = lambda wd: pl.BlockSpec((tl, wd), lambda b, l: (b * nl + l, 0))
    gl_spec = pl.BlockSpec((tl // c * heads, LANES), lambda b, l: (b * nl + l, 0))
    wide = jax.ShapeDtypeStruct((n, width), BF16)
    mats = jax.ShapeDtypeStruct((n, heads * c), F32)
    vk, qg, kt, qk, low, gl = pl.pallas_call(
        functools.partial(_dn_prep_body, tl=tl, heads=heads),
        grid=(bsz, nl),
        in_specs=[slab(0), slab(1), slab(2), rows(LANES), full(cw), full(alog), full(dtb)],
        out_specs=[rows(2 * width)] + [rows(width)] * 2 + [rows(heads * c)] * 2 + [gl_spec],
        out_shape=[jax.ShapeDtypeStruct((n, 2 * width), BF16)] + [wide] * 2 + [mats] * 2
        + [jax.ShapeDtypeStruct((chunks * heads, LANES), F32)],
        scratch_shapes=[pltpu.VMEM((3, tl + DN_HALO, width), F32)],
        compiler_params=_params(("arbitrary", "arbitrary"), 40),
        name="deltanet_prep",
    )(proj, proj, proj, tail, cw, alog, dtb)

    m_total = chunks * heads
    low_t = low.reshape(chunks, c, heads, c).transpose(1, 3, 0, 2).reshape(c, c, m_total)
    t_t = pl.pallas_call(
        _tri_inv_body,
        grid=(m_total // LANES,),
        in_specs=[pl.BlockSpec((c, c, LANES), lambda i: (0, 0, i))],
        out_specs=pl.BlockSpec((c, c, LANES), lambda i: (0, 0, i)),
        out_shape=jax.ShapeDtypeStruct((c, c, m_total), F32),
        compiler_params=_params(("parallel",), 40),
        name="deltanet_tri_inv",
    )(low_t)
    t_mat = t_t.reshape(c, c, chunks, heads).transpose(2, 0, 3, 1).reshape(n, heads * c)

    return pl.pallas_call(
        functools.partial(_dn_rec_body, tl=tl, heads=heads),
        grid=(bsz, nl),
        in_specs=[rows(heads * c), rows(2 * width)] + [rows(width)] * 2
        + [rows(heads * c), gl_spec, slab(3), full(ng)],
        out_specs=rows(width),
        out_shape=wide,
        scratch_shapes=[pltpu.VMEM((heads, DN_HEAD_DIM, DN_HEAD_DIM), F32)],
        compiler_params=_params(("arbitrary", "arbitrary"), 40),
        name="deltanet_rec",
    )(t_mat, vk, qg, kt, qk, gl, proj, ng)


def _out_body(y0_ref, y1_ref, y2_ref, y3_ref, w_ref, x_ref, g_ref, b_ref, o_ref, ob_ref, *, alpha):
    width = y0_ref.shape[1]
    acc = alpha * x_ref[...]
    for i, ref in enumerate((y0_ref, y1_ref, y2_ref, y3_ref)):
        acc = acc + _dot(ref[...], w_ref[i * width:(i + 1) * width, :])
    y = _layer_norm(acc, g_ref[...], b_ref[...])
    o_ref[...] = y
    ob_ref[...] = y.astype(BF16)


def _out_proj(ys, w_out, x, g, b, alpha, tm=512):
    n, d = x.shape
    width = ys[0].shape[1]
    tm = min(tm, n)
    row = lambda wd: pl.BlockSpec((tm, wd), lambda i: (i, 0))
    full = lambda a: pl.BlockSpec(a.shape, lambda i: (0,) * a.ndim)
    g = g.astype(F32).reshape(1, d)
    b = b.astype(F32).reshape(1, d)
    return pl.pallas_call(
        functools.partial(_out_body, alpha=alpha),
        grid=(n // tm,),
        in_specs=[row(width)] * 4 + [full(w_out), row(d), full(g), full(b)],
        out_specs=[row(d), row(d)],
        out_shape=[jax.ShapeDtypeStruct((n, d), F32), jax.ShapeDtypeStruct((n, d), BF16)],
        compiler_params=_params(("parallel",), 56),
        name="out_proj_ln",
    )(*ys, w_out, x, g, b)


def _ffn_body(xb_ref, wu_ref, wd_ref, x_ref, g_ref, b_ref, o_ref, ob_ref, acc, *, alpha):
    f = pl.program_id(1)

    @pl.when(f == 0)
    def _():
        acc[...] = alpha * x_ref[...]

    hidden = jnp.maximum(_dot(xb_ref[...], wu_ref[...]), 0.0)
    acc[...] += _dot((hidden * hidden).astype(BF16), wd_ref[...])

    @pl.when(f == pl.num_programs(1) - 1)
    def _():
        y = _layer_norm(acc[...], g_ref[...], b_ref[...])
        o_ref[...] = y
        ob_ref[...] = y.astype(BF16)


def _ffn(xb, x, w_up, w_down, g, b, alpha, tm=512, tf=512):
    n, d = x.shape
    ff = w_up.shape[1]
    tm = min(tm, n)
    g = g.astype(F32).reshape(1, d)
    b = b.astype(F32).reshape(1, d)
    row = pl.BlockSpec((tm, d), lambda i, f: (i, 0))
    vec = pl.BlockSpec((1, d), lambda i, f: (0, 0))
    return pl.pallas_call(
        functools.partial(_ffn_body, alpha=alpha),
        grid=(n // tm, ff // tf),
        in_specs=[row, pl.BlockSpec((d, tf), lambda i, f: (0, f)), pl.BlockSpec((tf, d), lambda i, f: (f, 0)),
                  row, vec, vec],
        out_specs=[row, row],
        out_shape=[jax.ShapeDtypeStruct((n, d), F32), jax.ShapeDtypeStruct((n, d), BF16)],
        scratch_shapes=[pltpu.VMEM((tm, d), F32)],
        compiler_params=_params(("parallel", "arbitrary"), 56),
        name="ffn_ln",
    )(xb, w_up, w_down, x, g, b)


def kernel(x, w_in, s5_lambda_re, s5_lambda_im, s5_log_step, s5_b_re, s5_b_im, s5_c_re, s5_c_im, s5_d, s5_glu_w, s5_glu_b, sgu_norm_g, sgu_norm_b, sgu_w, sgu_b, pool_w, pool_scale, dn_conv_w, dn_a_log, dn_dt_bias, dn_norm_g, w_out, ln1_g, ln1_b, w_up, w_down, ln2_g, ln2_b):
    bsz, seq, d = x.shape
    depth = w_in.shape[0]
    n = bsz * seq
    width = s5_glu_w.shape[1]
    heads = dn_a_log.shape[1]
    main_cols = 8 * width
    alpha = (2 * depth) ** 0.25
    s5_tl = min(256, seq)

    xf = x.reshape(n, d).astype(F32)
    xb = xf.astype(BF16)
    for i in range(depth):
        w_main = w_in[i, :, :main_cols].astype(BF16)
        w_tail = jnp.pad(w_in[i, :, main_cols:], ((0, 0), (0, LANES - 2 * heads))).astype(BF16)
        proj, tail = _proj(xb, w_main, w_tail)
        prep = _s5_prepare(s5_lambda_re[i], s5_lambda_im[i], s5_log_step[i], s5_b_re[i], s5_b_im[i],
                           s5_c_re[i], s5_c_im[i], s5_d[i], s5_tl // SUBLANES)
        y_s5 = _s5(proj, bsz, seq, prep, s5_glu_w[i], s5_glu_b[i], tl=s5_tl)
        y_sgu = _sgu(proj, n, width, sgu_norm_g[i], sgu_norm_b[i], sgu_w[i], sgu_b[i])
        y_pool = _pool(proj, bsz, seq, width, pool_w[i], pool_scale[i], col_block=3)
        y_dn = _dn(proj, tail, bsz, seq, width, 4, dn_conv_w[i], dn_a_log[i], dn_dt_bias[i], dn_norm_g[i])
        xf, xb = _out_proj((y_s5, y_sgu, y_pool, y_dn), w_out[i].astype(BF16), xf, ln1_g[i], ln1_b[i], alpha)
        xf, xb = _ffn(xb, xf, w_up[i].astype(BF16), w_down[i].astype(BF16), ln2_g[i], ln2_b[i], alpha)
    return xf.reshape(bsz, seq, d).astype(x.dtype)
```

```python
import functools
import math

import jax
import jax.numpy as jnp
from jax import lax
from jax.experimental import pallas as pl
from jax.experimental.pallas import tpu as pltpu

F32 = jnp.float32
BF16 = jnp.bfloat16
HIGHEST = lax.Precision.HIGHEST

LANES = 128
SUBLANES = 8

S5_CH_PER_GROUP = 16
S5_STATE = 64
S5_GROUPS_PER_SLAB = LANES // S5_CH_PER_GROUP
S5_SLAB_STATE = S5_GROUPS_PER_SLAB * S5_STATE
SGU_CHUNK = 128
SGU_HEAD_DIM = 64
POOL_WINDOWS = (2, 4, 8, 16)
POOL_HALO = 16
DN_HEAD_DIM = 128
DN_CONV = 4
DN_CHUNK = 64
DN_HALO = 8
LN_EPS = 1e-5
RMS_EPS = 1e-6
L2_EPS = 1e-6


def _params(semantics, vmem_mib):
    return pltpu.CompilerParams(dimension_semantics=semantics, vmem_limit_bytes=vmem_mib * 1024 * 1024)


def _layer_norm(y, g, b):
    mu = jnp.mean(y, axis=-1, keepdims=True)
    yc = y - mu
    var = jnp.mean(yc * yc, axis=-1, keepdims=True)
    return yc * lax.rsqrt(var + LN_EPS) * g + b


def _dot(a, b):
    return jnp.dot(a, b, preferred_element_type=F32)


def _dot_nt(a, b, precision=None):
    return lax.dot_general(a, b, (((1,), (1,)), ((), ())), precision=precision, preferred_element_type=F32)


def _dot_tn(a, b):
    return lax.dot_general(a, b, (((0,), (0,)), ((), ())), preferred_element_type=F32)


def _split3(x):
    hi = x.astype(BF16)
    rest = x - hi.astype(F32)
    mid = rest.astype(BF16)
    lo = (rest - mid.astype(F32)).astype(BF16)
    return hi, mid, lo


def _proj_body(x_ref, w_ref, wt_ref, o_ref, ot_ref):
    xb = x_ref[...]
    o_ref[...] = _dot(xb, w_ref[...])

    @pl.when(pl.program_id(1) == 0)
    def _():
        ot_ref[...] = _dot(xb, wt_ref[...])


def _proj(xb, w, cols, tm=1024, tn=512):
    n, d = xb.shape
    tm = min(tm, n)
    return pl.pallas_call(
        _proj_body,
        grid=(n // tm, cols // tn),
        in_specs=[pl.BlockSpec((tm, d), lambda i, j: (i, 0)),
                  pl.BlockSpec((d, tn), lambda i, j: (0, j)),
                  pl.BlockSpec((d, LANES), lambda i, j: (0, cols // LANES))],
        out_specs=[pl.BlockSpec((tm, tn), lambda i, j: (i, j)),
                   pl.BlockSpec((tm, LANES), lambda i, j: (i, 0))],
        out_shape=[jax.ShapeDtypeStruct((n, cols), F32), jax.ShapeDtypeStruct((n, LANES), F32)],
        compiler_params=_params(("parallel", "arbitrary"), 40),
        name="proj",
    )(xb, w, w)


def _cast_body(w_ref, o_ref, *, valid_cols):
    w = w_ref[...]
    if valid_cols < w.shape[1]:
        col = lax.broadcasted_iota(jnp.int32, w.shape, 1)
        w = jnp.where(col < valid_cols, w, 0.0)
    o_ref[...] = w.astype(o_ref.dtype)


def _cast_bf16(w, layer, block_rows, pad_cols_to=None):
    _, r, c = w.shape
    out_c = pad_cols_to or c
    return pl.pallas_call(
        functools.partial(_cast_body, valid_cols=c),
        grid=(r // block_rows,),
        in_specs=[pl.BlockSpec((None, block_rows, out_c), lambda i: (layer, i, 0))],
        out_specs=pl.BlockSpec((block_rows, out_c), lambda i: (i, 0)),
        out_shape=jax.ShapeDtypeStruct((r, out_c), BF16),
        compiler_params=_params(("parallel",), 40),
        name="cast_bf16",
    )(w)


def _s5_body(u_ref, perm_ref, unperm_ref, wb_ref, wc_ref, are_ref, aim_ref, pre_ref, pim_ref, d_ref, gw_ref, gb_ref,
             o_ref, sre, sim, hre, him, y_scr, *, tl, steps):
    n_slabs = u_ref.shape[1] // LANES
    sw = S5_SLAB_STATE
    tiles = sw // LANES

    @pl.when(pl.program_id(1) == 0)
    def _():
        hre[...] = jnp.zeros_like(hre)
        him[...] = jnp.zeros_like(him)

    u_nat = u_ref[...]
    perm = perm_ref[...]
    u1, u2, u3 = _split3(u_nat)
    u = _dot(perm, u1) + _dot(perm, u2) + _dot(perm, u3)
    ub = u.astype(BF16)

    for j in range(n_slabs):
        bu = _dot(ub[:, j * LANES:(j + 1) * LANES], wb_ref[j])
        for k in range(tiles):
            sre[j * tiles + k] = bu[:, k * LANES:(k + 1) * LANES]
            sim[j * tiles + k] = bu[:, sw + k * LANES:sw + (k + 1) * LANES]

    group = tiles
    for j in range(n_slabs):
        ts = [j * tiles + k for k in range(tiles)]
        cols = [slice(t * LANES, (t + 1) * LANES) for t in ts]

        ar = [jnp.broadcast_to(are_ref[:, c], (SUBLANES, LANES)) for c in cols]
        ai = [jnp.broadcast_to(aim_ref[:, c], (SUBLANES, LANES)) for c in cols]
        hr = [jnp.zeros((SUBLANES, LANES), F32)] * group
        hi = [jnp.zeros((SUBLANES, LANES), F32)] * group
        for r in range(steps):
            rows = slice(r * SUBLANES, (r + 1) * SUBLANES)
            for k, t in enumerate(ts):
                hr[k], hi[k] = (ar[k] * hr[k] - ai[k] * hi[k] + sre[t, rows, :],
                                ar[k] * hi[k] + ai[k] * hr[k] + sim[t, rows, :])
                sre[t, rows, :] = hr[k]
                sim[t, rows, :] = hi[k]

        cr8, ci8 = [], []
        for k in range(group):
            apr = pre_ref[steps - 1:steps, cols[k]]
            api = pim_ref[steps - 1:steps, cols[k]]
            cr = hre[:, cols[k]]
            ci = him[:, cols[k]]
            crs, cis = [], []
            for s in range(SUBLANES):
                crs.append(cr)
                cis.append(ci)
                cr, ci = (apr * cr - api * ci + hr[k][s:s + 1, :], apr * ci + api * cr + hi[k][s:s + 1, :])
            hre[:, cols[k]] = cr
            him[:, cols[k]] = ci
            cr8.append(jnp.concatenate(crs, axis=0))
            ci8.append(jnp.concatenate(cis, axis=0))
        for r in range(steps):
            rows = slice(r * SUBLANES, (r + 1) * SUBLANES)
            for k, t in enumerate(ts):
                pr = jnp.broadcast_to(pre_ref[r:r + 1, cols[k]], (SUBLANES, LANES))
                pi = jnp.broadcast_to(pim_ref[r:r + 1, cols[k]], (SUBLANES, LANES))
                sre[t, rows, :] = sre[t, rows, :] + (pr * cr8[k] - pi * ci8[k])
                sim[t, rows, :] = sim[t, rows, :] + (pr * ci8[k] + pi * cr8[k])

        h_re = jnp.concatenate([sre[t] for t in ts], axis=1).astype(BF16)
        h_im = jnp.concatenate([sim[t] for t in ts], axis=1).astype(BF16)
        y = _dot(h_re, wc_ref[j, :sw, :]) + _dot(h_im, wc_ref[j, sw:, :])
        lc = slice(j * LANES, (j + 1) * LANES)
        y = y + d_ref[:, lc] * u[:, lc]
        y_scr[:, lc] = jax.nn.gelu(y)

    y = y_scr[...]
    gate = _dot(y.astype(BF16), gw_ref[...]) + gb_ref[...]
    out = (y * jax.nn.sigmoid(gate)).astype(BF16)
    o_ref[...] = _dot(unperm_ref[...], out).astype(o_ref.dtype)


def _s5_prepare(lam_re, lam_im, log_step, b_re, b_im, c_re, c_im, d, steps):
    g, p = lam_re.shape
    h = S5_CH_PER_GROUP
    q = S5_GROUPS_PER_SLAB
    n_slabs = g // q
    step = jnp.exp(log_step.astype(F32))[:, None]
    lr, li = lam_re.astype(F32), lam_im.astype(F32)
    mag = jnp.exp(lr * step)
    a_re, a_im = mag * jnp.cos(li * step), mag * jnp.sin(li * step)
    den = lr * lr + li * li
    f_re = ((a_re - 1.0) * lr + a_im * li) / den
    f_im = (a_im * lr - (a_re - 1.0) * li) / den
    bb_re = f_re[:, :, None] * b_re - f_im[:, :, None] * b_im
    bb_im = f_re[:, :, None] * b_im + f_im[:, :, None] * b_re
    eye = jnp.eye(q, dtype=F32)

    def in_block(m):
        m = m.reshape(n_slabs, q, p, h)
        return jnp.einsum('jqph,qr->jqhrp', m, eye).reshape(n_slabs, q * h, q * p)

    def out_block(m):
        m = m.reshape(n_slabs, q, h, p)
        return jnp.einsum('jqhp,qr->jqprh', m, eye).reshape(n_slabs, q * p, q * h)

    wb = jnp.concatenate([in_block(bb_re), in_block(bb_im)], axis=2).astype(BF16)
    wc = jnp.concatenate([out_block(c_re.astype(F32)), out_block(-c_im.astype(F32))], axis=1).astype(BF16)
    k = jnp.arange(1, steps + 1, dtype=F32)[:, None, None]
    pmag = jnp.exp(lr * step * k)
    ang = li * step * k
    pw_re = (pmag * jnp.cos(ang)).reshape(steps, g * p)
    pw_im = (pmag * jnp.sin(ang)).reshape(steps, g * p)
    return (wb, wc, a_re.reshape(1, g * p), a_im.reshape(1, g * p), pw_re, pw_im,
            d.astype(F32).reshape(1, g * h))


def _s5(proj, bsz, seq, prep, glu_w, glu_b, tl=256):
    wb, wc, a_re, a_im, pw_re, pw_im, d = prep
    tl = min(tl, seq)
    steps = tl // SUBLANES
    width = d.shape[1]
    states = pw_re.shape[1]
    nl = seq // tl
    full = lambda a: pl.BlockSpec(a.shape, lambda b, l: (0,) * a.ndim)
    gw = glu_w.astype(BF16)
    gb = glu_b.astype(F32).reshape(1, width)
    dest = jnp.arange(tl)
    src = (dest % SUBLANES) * steps + dest // SUBLANES
    perm = (src[:, None] == jnp.arange(tl)[None, :]).astype(BF16)
    unperm = perm.T
    return pl.pallas_call(
        functools.partial(_s5_body, tl=tl, steps=steps),
        grid=(bsz, nl),
        in_specs=[pl.BlockSpec((tl, width), lambda b, l: (b * nl + l, 0)), full(perm), full(unperm),
                  full(wb), full(wc), full(a_re), full(a_im), full(pw_re), full(pw_im), full(d), full(gw), full(gb)],
        out_specs=pl.BlockSpec((tl, width), lambda b, l: (b * nl + l, 0)),
        out_shape=jax.ShapeDtypeStruct((bsz * seq, width), BF16),
        scratch_shapes=[pltpu.VMEM((states // LANES, tl, LANES), F32), pltpu.VMEM((states // LANES, tl, LANES), F32),
                        pltpu.VMEM((1, states), F32), pltpu.VMEM((1, states), F32),
                        pltpu.VMEM((tl, width), F32)],
        compiler_params=_params(("arbitrary", "arbitrary"), 40),
        name="s5",
    )(proj, perm, unperm, wb, wc, a_re, a_im, pw_re, pw_im, d, gw, gb)


def _sgu_body(zu_ref, zv_ref, g_ref, b_ref, w_ref, bias_ref, o_ref, *, tl):
    width = zu_ref.shape[1]
    u = jax.nn.gelu(zu_ref[...])
    v = _layer_norm(jax.nn.gelu(zv_ref[...]), g_ref[...], b_ref[...])
    vb = v.astype(BF16)
    row = lax.broadcasted_iota(jnp.int32, (2 * SGU_CHUNK, SGU_CHUNK), 0)
    col = lax.broadcasted_iota(jnp.int32, (2 * SGU_CHUNK, SGU_CHUNK), 1)
    causal = col <= (row % SGU_CHUNK)
    lane = lax.broadcasted_iota(jnp.int32, (SGU_CHUNK, LANES), 1)
    first = lane < SGU_HEAD_DIM
    for j in range(width // LANES):
        w = jnp.where(causal, w_ref[j], 0.0).astype(BF16)
        bias = bias_ref[j]
        for c in range(tl // SGU_CHUNK):
            rs = slice(c * SGU_CHUNK, (c + 1) * SGU_CHUNK)
            cs = slice(j * LANES, (j + 1) * LANES)
            r = _dot(w, vb[rs, cs])
            mixed = jnp.where(first, r[:SGU_CHUNK], r[SGU_CHUNK:]) + bias
            o_ref[rs, cs] = (u[rs, cs] * mixed).astype(o_ref.dtype)


def _sgu(proj, n, width, norm_g, norm_b, w_s, b_s, tl=512):
    heads = w_s.shape[0]
    tl = min(tl, n)
    w_pairs = w_s.astype(F32).reshape(heads // 2, 2 * SGU_CHUNK, SGU_CHUNK)
    bias = jnp.repeat(b_s.astype(F32).T, SGU_HEAD_DIM, axis=1)
    bias = bias.reshape(SGU_CHUNK, heads // 2, LANES).transpose(1, 0, 2)
    g = norm_g.astype(F32).reshape(1, width)
    b = norm_b.astype(F32).reshape(1, width)
    full = lambda a: pl.BlockSpec(a.shape, lambda i: (0,) * a.ndim)
    return pl.pallas_call(
        functools.partial(_sgu_body, tl=tl),
        grid=(n // tl,),
        in_specs=[pl.BlockSpec((tl, width), lambda i: (i, 1)),
                  pl.BlockSpec((tl, width), lambda i: (i, 2)),
                  full(g), full(b), full(w_pairs), full(bias)],
        out_specs=pl.BlockSpec((tl, width), lambda i: (i, 0)),
        out_shape=jax.ShapeDtypeStruct((n, width), BF16),
        compiler_params=_params(("parallel",), 40),
        name="sgu",
    )(proj, proj, g, b, w_pairs, bias)


def _pool_body(p_ref, w_ref, sc_ref, o_ref, ext, *, tl):
    l = pl.program_id(1)

    @pl.when(l == 0)
    def _():
        ext[0:POOL_HALO, :] = jnp.zeros((POOL_HALO, ext.shape[1]), F32)

    @pl.when(l > 0)
    def _():
        ext[0:POOL_HALO, :] = ext[tl:tl + POOL_HALO, :]

    ext[POOL_HALO:POOL_HALO + tl, :] = p_ref[...]
    pos = l * tl + lax.broadcasted_iota(jnp.int32, (tl, LANES), 0)
    for gi, win in enumerate(POOL_WINDOWS):
        cs = slice(gi * LANES, (gi + 1) * LANES)
        x = ext[POOL_HALO:POOL_HALO + tl, cs]
        acc = ext[:, cs]
        span = 1
        while span < win:
            acc = acc + pltpu.roll(acc, span, 0)
            span *= 2
        acc = acc[POOL_HALO:, :]
        count = jnp.minimum(pos + 1, win).astype(F32)
        pooled = acc / count - x
        y = _dot(pooled.astype(BF16), w_ref[gi]) * sc_ref[:, cs]
        o_ref[:, cs] = y.astype(o_ref.dtype)


def _pool(proj, bsz, seq, width, w_pool, scale, col_block, tl=512):
    tl = min(tl, seq)
    nl = seq // tl
    w = w_pool.astype(BF16)
    sc = scale.astype(F32).reshape(1, width)
    return pl.pallas_call(
        functools.partial(_pool_body, tl=tl),
        grid=(bsz, nl),
        in_specs=[pl.BlockSpec((tl, width), lambda b, l: (b * nl + l, col_block)),
                  pl.BlockSpec(w.shape, lambda b, l: (0, 0, 0)),
                  pl.BlockSpec(sc.shape, lambda b, l: (0, 0))],
        out_specs=pl.BlockSpec((tl, width), lambda b, l: (b * nl + l, 0)),
        out_shape=jax.ShapeDtypeStruct((bsz * seq, width), BF16),
        scratch_shapes=[pltpu.VMEM((tl + POOL_HALO, width), F32)],
        compiler_params=_params(("arbitrary", "arbitrary"), 40),
        name="pool",
    )(proj, w, sc)


def _dn_prep_body(q_ref, k_ref, v_ref, tail_ref, cw_ref, alog_ref, dtb_ref,
                  vk_ref, qg_ref, kt_ref, qk_ref, low_ref, gl_ref, ext, *, tl, heads):
    l = pl.program_id(1)
    width = heads * DN_HEAD_DIM
    c = DN_CHUNK

    @pl.when(l == 0)
    def _():
        ext[:, 0:DN_HALO, :] = jnp.zeros((3, DN_HALO, width), F32)

    @pl.when(l > 0)
    def _():
        ext[:, 0:DN_HALO, :] = ext[:, tl:tl + DN_HALO, :]

    mixed = []
    for i, ref in enumerate((q_ref, k_ref, v_ref)):
        ext[i, DN_HALO:DN_HALO + tl, :] = ref[...]
        e = ext[i]
        acc = e * cw_ref[0:1, i * width:(i + 1) * width]
        for j in range(1, DN_CONV):
            acc = pltpu.roll(acc, 1, 0) + e * cw_ref[j:j + 1, i * width:(i + 1) * width]
        acc = acc[DN_HALO:, :]
        mixed.append(acc * jax.nn.sigmoid(acc))
    qc, kc, vc = mixed

    tail = tail_ref[...]
    sp_in = tail + dtb_ref[...]
    softplus = jnp.maximum(sp_in, 0.0) + jnp.log1p(jnp.exp(-jnp.abs(sp_in)))
    g_all = -jnp.exp(alog_ref[...]) * softplus
    beta_all = jax.nn.sigmoid(tail)

    ri = lax.broadcasted_iota(jnp.int32, (c, c), 0)
    ci = lax.broadcasted_iota(jnp.int32, (c, c), 1)
    causal = ri >= ci
    strict = ri > ci
    tri = causal.astype(BF16)
    sel = (lax.broadcasted_iota(jnp.int32, (heads * c, LANES), 0) // c
           == lax.broadcasted_iota(jnp.int32, (heads * c, LANES), 1)).astype(BF16)
    chunks = range(tl // c)

    gc = []
    for n in chunks:
        g1, g2, g3 = _split3(g_all[n * c:(n + 1) * c, :])
        gc.append(_dot(tri, g1) + _dot(tri, g2) + _dot(tri, g3))
    grow = []
    for n in chunks:
        g1, g2, g3 = _split3(gc[n])
        grow.append(_dot_nt(sel, g1) + _dot_nt(sel, g2) + _dot_nt(sel, g3))
    for n in chunks:
        rs = slice(n * c, (n + 1) * c)
        qn, kn, kbeta, kq = {}, {}, {}, {}
        for h in range(heads):
            hs = slice(h * DN_HEAD_DIM, (h + 1) * DN_HEAD_DIM)
            q = qc[rs, hs]
            k = kc[rs, hs]
            q = q * lax.rsqrt(jnp.sum(q * q, axis=-1, keepdims=True) + L2_EPS) * (DN_HEAD_DIM ** -0.5)
            k = k * lax.rsqrt(jnp.sum(k * k, axis=-1, keepdims=True) + L2_EPS)
            kb = k * beta_all[rs, heads + h:heads + h + 1]
            lhs = jnp.concatenate([kb.astype(BF16), q.astype(BF16)], axis=0)
            qn[n, h], kn[n, h], kbeta[n, h] = q, k, kb
            kq[n, h] = _dot_nt(lhs, k.astype(BF16))
        for h in range(heads):
            hs = slice(h * DN_HEAD_DIM, (h + 1) * DN_HEAD_DIM)
            ms = slice(h * c, (h + 1) * c)
            gcol = gc[n][:, h:h + 1]
            glast = gc[n][c - 1:c, h:h + 1]
            decay = jnp.exp(jnp.where(causal, gcol - grow[n][ms, :], -jnp.inf))
            egc = jnp.exp(gcol)
            low_ref[rs, ms] = jnp.where(strict, kq[n, h][:c] * decay, 0.0)
            qk_ref[rs, ms] = kq[n, h][c:] * decay
            beta = beta_all[rs, heads + h:heads + h + 1]
            vk_ref[rs, 2 * h * DN_HEAD_DIM:(2 * h + 1) * DN_HEAD_DIM] = (vc[rs, hs] * beta).astype(BF16)
            vk_ref[rs, (2 * h + 1) * DN_HEAD_DIM:(2 * h + 2) * DN_HEAD_DIM] = (kbeta[n, h] * egc).astype(BF16)
            qg_ref[rs, hs] = (qn[n, h] * egc).astype(BF16)
            kt_ref[rs, hs] = (kn[n, h] * jnp.exp(glast - gcol)).astype(BF16)
            gl_ref[n * heads + h:n * heads + h + 1, :] = jnp.broadcast_to(jnp.exp(glast), (1, LANES))


def _tri_inv_body(l_ref, t_ref):
    c = l_ref.shape[0]
    groups = c // SUBLANES
    sub = lax.broadcasted_iota(jnp.int32, (SUBLANES, LANES), 0)
    zero = jnp.zeros((SUBLANES, LANES), F32)
    for i in range(c):
        live = i // SUBLANES + 1
        acc = [zero] * live
        acc[live - 1] = (sub == i % SUBLANES).astype(F32)
        for j in range(i):
            lij = jnp.broadcast_to(l_ref[i, j:j + 1, :], (SUBLANES, LANES))
            for cg in range(j // SUBLANES + 1):
                acc[cg] = acc[cg] - lij * t_ref[j, cg * SUBLANES:(cg + 1) * SUBLANES, :]
        for cg in range(groups):
            t_ref[i, cg * SUBLANES:(cg + 1) * SUBLANES, :] = acc[cg] if cg < live else zero


def _dn_rec_body(t_ref, vk_ref, qg_ref, kt_ref, qk_ref, gl_ref, gate_ref, ng_ref, o_ref, state,
                 *, tl, heads):
    c = DN_CHUNK
    dk = DN_HEAD_DIM

    @pl.when(pl.program_id(1) == 0)
    def _():
        state[...] = jnp.zeros_like(state)

    chunks = range(tl // c)
    rows = lambda n: slice(n * c, (n + 1) * c)
    hcol = lambda h: slice(h * dk, (h + 1) * dk)
    mcol = lambda h: slice(h * c, (h + 1) * c)

    uw, uwb, kuw = {}, {}, {}
    for n in chunks:
        for h in range(heads):
            uw[n, h] = _dot(t_ref[rows(n), mcol(h)].astype(BF16), vk_ref[rows(n), 2 * h * dk:2 * (h + 1) * dk])
            uwb[n, h] = uw[n, h].astype(BF16)
    for n in chunks:
        for h in range(heads):
            kuw[n, h] = _dot_tn(kt_ref[rows(n), hcol(h)], uwb[n, h])

    def emit(n, h, wq, v_bf16):
        o = wq[c:] + _dot(qk_ref[rows(n), mcol(h)].astype(BF16), v_bf16)
        o = o * lax.rsqrt(jnp.mean(o * o, axis=-1, keepdims=True) + RMS_EPS) * ng_ref[...]
        gt = gate_ref[rows(n), hcol(h)]
        o_ref[rows(n), hcol(h)] = (o * (gt * jax.nn.sigmoid(gt))).astype(o_ref.dtype)

    s = [state[h] for h in range(heads)]
    pending = []
    for n in chunks:
        sb = [s[h].astype(BF16) for h in range(heads)]
        for h in range(heads):
            s[h] = (s[h] * gl_ref[n * heads + h:n * heads + h + 1, :] + kuw[n, h][:, :dk]
                    - _dot(kuw[n, h][:, dk:].astype(BF16), sb[h]))
        wq = []
        for h in range(heads):
            lhs = jnp.concatenate([uwb[n, h][:, dk:], qg_ref[rows(n), hcol(h)]], axis=0)
            wq.append(_dot(lhs, sb[h]))
        for args in pending:
            emit(*args)
        pending = [(n, h, wq[h], (uw[n, h][:, :dk] - wq[h][:c]).astype(BF16)) for h in range(heads)]
    for args in pending:
        emit(*args)
    for h in range(heads):
        state[h] = s[h]


def _dn(proj, tail, bsz, seq, width, col_block, conv_w, a_log, dt_bias, norm_g, tl=256):
    heads = width // DN_HEAD_DIM
    c = DN_CHUNK
    n = bsz * seq
    tl = min(tl, seq)
    nl = seq // tl
    chunks = n // c
    cw = conv_w.astype(F32)
    pad = lambda a: jnp.pad(a.astype(F32), (0, LANES - heads)).reshape(1, LANES)
    alog = pad(a_log)
    dtb = pad(dt_bias)
    ng = norm_g.astype(F32).reshape(1, DN_HEAD_DIM)
    full = lambda a: pl.BlockSpec(a.shape, lambda b, l: (0,) * a.ndim)
    slab = lambda k: pl.BlockSpec((tl, width), lambda b, l: (b * nl + l, col_block + k))
    rows = lambda wd: pl.BlockSpec((tl, wd), lambda b, l: (b * nl + l, 0))
    gl_spec = pl.BlockSpec((tl // c * heads, LANES), lambda b, l: (b * nl + l, 0))
    wide = jax.ShapeDtypeStruct((n, width), BF16)
    mats = jax.ShapeDtypeStruct((n, heads * c), F32)
    vk, qg, kt, qk, low, gl = pl.pallas_call(
        functools.partial(_dn_prep_body, tl=tl, heads=heads),
        grid=(bsz, nl),
        in_specs=[slab(0), slab(1), slab(2), rows(LANES), full(cw), full(alog), full(dtb)],
        out_specs=[rows(2 * width)] + [rows(width)] * 2 + [rows(heads * c)] * 2 + [gl_spec],
        out_shape=[jax.ShapeDtypeStruct((n, 2 * width), BF16)] + [wide] * 2 + [mats] * 2
        + [jax.ShapeDtypeStruct((chunks * heads, LANES), F32)],
        scratch_shapes=[pltpu.VMEM((3, tl + DN_HALO, width), F32)],
        compiler_params=_params(("arbitrary", "arbitrary"), 40),
        name="deltanet_prep",
    )(proj, proj, proj, tail, cw, alog, dtb)

    m_total = chunks * heads
    low_t = low.reshape(chunks, c, heads, c).transpose(1, 3, 0, 2).reshape(c, c, m_total)
    t_t = pl.pallas_call(
        _tri_inv_body,
        grid=(m_total // LANES,),
        in_specs=[pl.BlockSpec((c, c, LANES), lambda i: (0, 0, i))],
        out_specs=pl.BlockSpec((c, c, LANES), lambda i: (0, 0, i)),
        out_shape=jax.ShapeDtypeStruct((c, c, m_total), F32),
        compiler_params=_params(("parallel",), 40),
        name="deltanet_tri_inv",
    )(low_t)
    t_mat = t_t.reshape(c, c, chunks, heads).transpose(2, 0, 3, 1).reshape(n, heads * c)

    return pl.pallas_call(
        functools.partial(_dn_rec_body, tl=tl, heads=heads),
        grid=(bsz, nl),
        in_specs=[rows(heads * c), rows(2 * width)] + [rows(width)] * 2
        + [rows(heads * c), gl_spec, slab(3), full(ng)],
        out_specs=rows(width),
        out_shape=wide,
        scratch_shapes=[pltpu.VMEM((heads, DN_HEAD_DIM, DN_HEAD_DIM), F32)],
        compiler_params=_params(("arbitrary", "arbitrary"), 40),
        name="deltanet_rec",
    )(t_mat, vk, qg, kt, qk, gl, proj, ng)


def _out_body(y0_ref, y1_ref, y2_ref, y3_ref, w_ref, x_ref, g_ref, b_ref, o_ref, ob_ref, *, alpha):
    width = y0_ref.shape[1]
    half = x_ref.shape[0] // 2
    accs = []
    for rows in (slice(0, half), slice(half, 2 * half)):
        mixed = jnp.concatenate([ref[rows, :] for ref in (y0_ref, y1_ref, y2_ref, y3_ref)], axis=1)
        accs.append((rows, alpha * x_ref[rows, :] + _dot(mixed, w_ref[...])))
    for rows, acc in accs:
        y = _layer_norm(acc, g_ref[...], b_ref[...])
        o_ref[rows, :] = y
        ob_ref[rows, :] = y.astype(BF16)


def _out_proj(ys, w_out, x, g, b, alpha, tm=512):
    n, d = x.shape
    width = ys[0].shape[1]
    tm = min(tm, n)
    row = lambda wd: pl.BlockSpec((tm, wd), lambda i: (i, 0))
    full = lambda a: pl.BlockSpec(a.shape, lambda i: (0,) * a.ndim)
    g = g.astype(F32).reshape(1, d)
    b = b.astype(F32).reshape(1, d)
    return pl.pallas_call(
        functools.partial(_out_body, alpha=alpha),
        grid=(n // tm,),
        in_specs=[row(width)] * 4 + [full(w_out), row(d), full(g), full(b)],
        out_specs=[row(d), row(d)],
        out_shape=[jax.ShapeDtypeStruct((n, d), F32), jax.ShapeDtypeStruct((n, d), BF16)],
        compiler_params=_params(("parallel",), 56),
        name="out_proj_ln",
    )(*ys, w_out, x, g, b)


def _ffn_body(xb_ref, wu_ref, wd_ref, x_ref, g_ref, b_ref, o_ref, ob_ref, acc, *, alpha):
    f = pl.program_id(1)

    @pl.when(f == 0)
    def _():
        acc[...] = alpha * x_ref[...]

    hidden = jnp.maximum(_dot(xb_ref[...], wu_ref[...]), 0.0)
    acc[...] += _dot((hidden * hidden).astype(BF16), wd_ref[...])

    @pl.when(f == pl.num_programs(1) - 1)
    def _():
        y = _layer_norm(acc[...], g_ref[...], b_ref[...])
        o_ref[...] = y
        ob_ref[...] = y.astype(BF16)


def _ffn(xb, x, w_up, w_down, g, b, alpha, tm=512, tf=1024):
    n, d = x.shape
    ff = w_up.shape[1]
    tm = min(tm, n)
    g = g.astype(F32).reshape(1, d)
    b = b.astype(F32).reshape(1, d)
    row = pl.BlockSpec((tm, d), lambda i, f: (i, 0))
    vec = pl.BlockSpec((1, d), lambda i, f: (0, 0))
    return pl.pallas_call(
        functools.partial(_ffn_body, alpha=alpha),
        grid=(n // tm, ff // tf),
        in_specs=[row, pl.BlockSpec((d, tf), lambda i, f: (0, f)), pl.BlockSpec((tf, d), lambda i, f: (f, 0)),
                  row, vec, vec],
        out_specs=[row, row],
        out_shape=[jax.ShapeDtypeStruct((n, d), F32), jax.ShapeDtypeStruct((n, d), BF16)],
        scratch_shapes=[pltpu.VMEM((tm, d), F32)],
        compiler_params=_params(("parallel", "arbitrary"), 56),
        name="ffn_ln",
    )(xb, w_up, w_down, x, g, b)


def kernel(x, w_in, s5_lambda_re, s5_lambda_im, s5_log_step, s5_b_re, s5_b_im, s5_c_re, s5_c_im, s5_d, s5_glu_w, s5_glu_b, sgu_norm_g, sgu_norm_b, sgu_w, sgu_b, pool_w, pool_scale, dn_conv_w, dn_a_log, dn_dt_bias, dn_norm_g, w_out, ln1_g, ln1_b, w_up, w_down, ln2_g, ln2_b):
    bsz, seq, d = x.shape
    depth = w_in.shape[0]
    n = bsz * seq
    width = s5_glu_w.shape[1]
    heads = dn_a_log.shape[1]
    main_cols = 8 * width
    alpha = (2 * depth) ** 0.25
    s5_tl = min(256, seq)

    xf = x.reshape(n, d).astype(F32)
    xb = _cast_bf16(xf.reshape(1, n, d), 0, min(512, n))
    for i in range(depth):
        proj, tail = _proj(xb, _cast_bf16(w_in, i, 256, pad_cols_to=main_cols + LANES), main_cols)
        prep = _s5_prepare(s5_lambda_re[i], s5_lambda_im[i], s5_log_step[i], s5_b_re[i], s5_b_im[i],
                           s5_c_re[i], s5_c_im[i], s5_d[i], s5_tl // SUBLANES)
        y_s5 = _s5(proj, bsz, seq, prep, s5_glu_w[i], s5_glu_b[i], tl=s5_tl)
        y_sgu = _sgu(proj, n, width, sgu_norm_g[i], sgu_norm_b[i], sgu_w[i], sgu_b[i])
        y_pool = _pool(proj, bsz, seq, width, pool_w[i], pool_scale[i], col_block=3)
        y_dn = _dn(proj, tail, bsz, seq, width, 4, dn_conv_w[i], dn_a_log[i], dn_dt_bias[i], dn_norm_g[i])
        xf, xb = _out_proj((y_s5, y_sgu, y_pool, y_dn), _cast_bf16(w_out, i, 1024), xf, ln1_g[i], ln1_b[i], alpha)
        xf, xb = _ffn(xb, xf, _cast_bf16(w_up, i, 256), _cast_bf16(w_down, i, 1024), ln2_g[i], ln2_b[i], alpha)
    return xf.reshape(bsz, seq, d).astype(x.dtype)
```

```python
import functools
import math

import jax
import jax.numpy as jnp
from jax import lax
from jax.experimental import pallas as pl
from jax.experimental.pallas import tpu as pltpu

F32 = jnp.float32
BF16 = jnp.bfloat16
HIGHEST = lax.Precision.HIGHEST

LANES = 128
SUBLANES = 8

S5_CH_PER_GROUP = 16
S5_STATE = 64
S5_GROUPS_PER_SLAB = LANES // S5_CH_PER_GROUP
S5_SLAB_STATE = S5_GROUPS_PER_SLAB * S5_STATE
SGU_CHUNK = 128
SGU_HEAD_DIM = 64
POOL_WINDOWS = (2, 4, 8, 16)
POOL_HALO = 16
DN_HEAD_DIM = 128
DN_CONV = 4
DN_CHUNK = 64
DN_HALO = 8
LN_EPS = 1e-5
RMS_EPS = 1e-6
L2_EPS = 1e-6


def _params(semantics, vmem_mib):
    return pltpu.CompilerParams(dimension_semantics=semantics, vmem_limit_bytes=vmem_mib * 1024 * 1024)


def _layer_norm(y, g, b):
    mu = jnp.mean(y, axis=-1, keepdims=True)
    yc = y - mu
    var = jnp.mean(yc * yc, axis=-1, keepdims=True)
    return yc * lax.rsqrt(var + LN_EPS) * g + b


def _dot(a, b):
    return jnp.dot(a, b, preferred_element_type=F32)


def _dot_nt(a, b, precision=None):
    return lax.dot_general(a, b, (((1,), (1,)), ((), ())), precision=precision, preferred_element_type=F32)


def _dot_tn(a, b):
    return lax.dot_general(a, b, (((0,), (0,)), ((), ())), preferred_element_type=F32)


def _split3(x):
    hi = x.astype(BF16)
    rest = x - hi.astype(F32)
    mid = rest.astype(BF16)
    lo = (rest - mid.astype(F32)).astype(BF16)
    return hi, mid, lo


def _proj_body(x_ref, w_ref, wt_ref, o_ref, ot_ref, *scratch):
    if scratch:
        xb_ref, = scratch

        @pl.when(pl.program_id(1) == 0)
        def _():
            xb_ref[...] = x_ref[...].astype(BF16)
    else:
        xb_ref = x_ref
    o_ref[...] = _dot_nt(xb_ref[...], w_ref[...])

    @pl.when(pl.program_id(1) == 0)
    def _():
        ot_ref[...] = _dot_nt(xb_ref[...], wt_ref[...])


def _proj(x, w_t, cols, tm=1024, tn=1024):
    n, d = x.shape
    tm = min(tm, n)
    scratch = [] if x.dtype == BF16 else [pltpu.VMEM((tm, d), BF16)]
    return pl.pallas_call(
        _proj_body,
        grid=(n // tm, cols // tn),
        in_specs=[pl.BlockSpec((tm, d), lambda i, j: (i, 0)),
                  pl.BlockSpec((tn, d), lambda i, j: (j, 0)),
                  pl.BlockSpec((LANES, d), lambda i, j: (cols // LANES, 0))],
        out_specs=[pl.BlockSpec((tm, tn), lambda i, j: (i, j)),
                   pl.BlockSpec((tm, LANES), lambda i, j: (i, 0))],
        out_shape=[jax.ShapeDtypeStruct((n, cols), F32), jax.ShapeDtypeStruct((n, LANES), F32)],
        scratch_shapes=scratch,
        compiler_params=_params(("parallel", "arbitrary"), 56),
        name="proj",
    )(x, w_t, w_t)


def _cast_body(w_ref, o_ref, *, valid_rows):
    w = w_ref[...]
    if valid_rows % w.shape[0]:
        row = pl.program_id(0) * w.shape[0] + lax.broadcasted_iota(jnp.int32, w.shape, 0)
        w = jnp.where(row < valid_rows, w, 0.0)
    o_ref[...] = w.astype(o_ref.dtype)


def _cast_bf16(w, layer, block_rows, pad_rows_to=None):
    _, r, c = w.shape
    out_r = pad_rows_to or r
    return pl.pallas_call(
        functools.partial(_cast_body, valid_rows=r),
        grid=(out_r // block_rows,),
        in_specs=[pl.BlockSpec((None, block_rows, c), lambda i: (layer, i, 0))],
        out_specs=pl.BlockSpec((block_rows, c), lambda i: (i, 0)),
        out_shape=jax.ShapeDtypeStruct((out_r, c), BF16),
        compiler_params=_params(("parallel",), 40),
        name="cast_bf16",
    )(w)


def _s5_body(u_ref, perm_ref, unperm_ref, wb_ref, wc_ref, are_ref, aim_ref, pre_ref, pim_ref, d_ref, gw_ref, gb_ref,
             o_ref, sre, sim, hre, him, y_scr, *, tl, steps):
    n_slabs = u_ref.shape[1] // LANES
    sw = S5_SLAB_STATE
    tiles = sw // LANES

    @pl.when(pl.program_id(1) == 0)
    def _():
        hre[...] = jnp.zeros_like(hre)
        him[...] = jnp.zeros_like(him)

    u_nat = u_ref[...]
    perm = perm_ref[...]
    u1, u2, u3 = _split3(u_nat)
    u = _dot(perm, u1) + _dot(perm, u2) + _dot(perm, u3)
    ub = u.astype(BF16)

    for j in range(n_slabs):
        bu = _dot(ub[:, j * LANES:(j + 1) * LANES], wb_ref[j])
        for k in range(tiles):
            sre[j * tiles + k] = bu[:, k * LANES:(k + 1) * LANES]
            sim[j * tiles + k] = bu[:, sw + k * LANES:sw + (k + 1) * LANES]

    group = tiles
    for j in range(n_slabs):
        ts = [j * tiles + k for k in range(tiles)]
        cols = [slice(t * LANES, (t + 1) * LANES) for t in ts]

        ar = [jnp.broadcast_to(are_ref[:, c], (SUBLANES, LANES)) for c in cols]
        ai = [jnp.broadcast_to(aim_ref[:, c], (SUBLANES, LANES)) for c in cols]
        hr = [jnp.zeros((SUBLANES, LANES), F32)] * group
        hi = [jnp.zeros((SUBLANES, LANES), F32)] * group
        for r in range(steps):
            rows = slice(r * SUBLANES, (r + 1) * SUBLANES)
            for k, t in enumerate(ts):
                hr[k], hi[k] = (ar[k] * hr[k] - ai[k] * hi[k] + sre[t, rows, :],
                                ar[k] * hi[k] + ai[k] * hr[k] + sim[t, rows, :])
                sre[t, rows, :] = hr[k]
                sim[t, rows, :] = hi[k]

        cr8, ci8 = [], []
        for k in range(group):
            apr = pre_ref[steps - 1:steps, cols[k]]
            api = pim_ref[steps - 1:steps, cols[k]]
            cr = hre[:, cols[k]]
            ci = him[:, cols[k]]
            crs, cis = [], []
            for s in range(SUBLANES):
                crs.append(cr)
                cis.append(ci)
                cr, ci = (apr * cr - api * ci + hr[k][s:s + 1, :], apr * ci + api * cr + hi[k][s:s + 1, :])
            hre[:, cols[k]] = cr
            him[:, cols[k]] = ci
            cr8.append(jnp.concatenate(crs, axis=0))
            ci8.append(jnp.concatenate(cis, axis=0))
        for r in range(steps):
            rows = slice(r * SUBLANES, (r + 1) * SUBLANES)
            for k, t in enumerate(ts):
                pr = jnp.broadcast_to(pre_ref[r:r + 1, cols[k]], (SUBLANES, LANES))
                pi = jnp.broadcast_to(pim_ref[r:r + 1, cols[k]], (SUBLANES, LANES))
                sre[t, rows, :] = sre[t, rows, :] + (pr * cr8[k] - pi * ci8[k])
                sim[t, rows, :] = sim[t, rows, :] + (pr * ci8[k] + pi * cr8[k])

        h_re = jnp.concatenate([sre[t] for t in ts], axis=1).astype(BF16)
        h_im = jnp.concatenate([sim[t] for t in ts], axis=1).astype(BF16)
        y = _dot(h_re, wc_ref[j, :sw, :]) + _dot(h_im, wc_ref[j, sw:, :])
        lc = slice(j * LANES, (j + 1) * LANES)
        y = y + d_ref[:, lc] * u[:, lc]
        y_scr[:, lc] = jax.nn.gelu(y)

    y = y_scr[...]
    gate = _dot(y.astype(BF16), gw_ref[...]) + gb_ref[...]
    out = (y * jax.nn.sigmoid(gate)).astype(BF16)
    o_ref[...] = _dot(unperm_ref[...], out).astype(o_ref.dtype)


def _s5_prepare(lam_re, lam_im, log_step, b_re, b_im, c_re, c_im, d, steps):
    g, p = lam_re.shape
    h = S5_CH_PER_GROUP
    q = S5_GROUPS_PER_SLAB
    n_slabs = g // q
    step = jnp.exp(log_step.astype(F32))[:, None]
    lr, li = lam_re.astype(F32), lam_im.astype(F32)
    mag = jnp.exp(lr * step)
    a_re, a_im = mag * jnp.cos(li * step), mag * jnp.sin(li * step)
    den = lr * lr + li * li
    f_re = ((a_re - 1.0) * lr + a_im * li) / den
    f_im = (a_im * lr - (a_re - 1.0) * li) / den
    bb_re = f_re[:, :, None] * b_re - f_im[:, :, None] * b_im
    bb_im = f_re[:, :, None] * b_im + f_im[:, :, None] * b_re
    eye = jnp.eye(q, dtype=F32)

    def in_block(m):
        m = m.reshape(n_slabs, q, p, h)
        return jnp.einsum('jqph,qr->jqhrp', m, eye).reshape(n_slabs, q * h, q * p)

    def out_block(m):
        m = m.reshape(n_slabs, q, h, p)
        return jnp.einsum('jqhp,qr->jqprh', m, eye).reshape(n_slabs, q * p, q * h)

    wb = jnp.concatenate([in_block(bb_re), in_block(bb_im)], axis=2).astype(BF16)
    wc = jnp.concatenate([out_block(c_re.astype(F32)), out_block(-c_im.astype(F32))], axis=1).astype(BF16)
    k = jnp.arange(1, steps + 1, dtype=F32)[:, None, None]
    pmag = jnp.exp(lr * step * k)
    ang = li * step * k
    pw_re = (pmag * jnp.cos(ang)).reshape(steps, g * p)
    pw_im = (pmag * jnp.sin(ang)).reshape(steps, g * p)
    return (wb, wc, a_re.reshape(1, g * p), a_im.reshape(1, g * p), pw_re, pw_im,
            d.astype(F32).reshape(1, g * h))


def _s5(proj, bsz, seq, prep, glu_w, glu_b, tl=256):
    wb, wc, a_re, a_im, pw_re, pw_im, d = prep
    tl = min(tl, seq)
    steps = tl // SUBLANES
    width = d.shape[1]
    states = pw_re.shape[1]
    nl = seq // tl
    full = lambda a: pl.BlockSpec(a.shape, lambda b, l: (0,) * a.ndim)
    gw = glu_w.astype(BF16)
    gb = glu_b.astype(F32).reshape(1, width)
    dest = jnp.arange(tl)
    src = (dest % SUBLANES) * steps + dest // SUBLANES
    perm = (src[:, None] == jnp.arange(tl)[None, :]).astype(BF16)
    unperm = perm.T
    return pl.pallas_call(
        functools.partial(_s5_body, tl=tl, steps=steps),
        grid=(bsz, nl),
        in_specs=[pl.BlockSpec((tl, width), lambda b, l: (b * nl + l, 0)), full(perm), full(unperm),
                  full(wb), full(wc), full(a_re), full(a_im), full(pw_re), full(pw_im), full(d), full(gw), full(gb)],
        out_specs=pl.BlockSpec((tl, width), lambda b, l: (b * nl + l, 0)),
        out_shape=jax.ShapeDtypeStruct((bsz * seq, width), BF16),
        scratch_shapes=[pltpu.VMEM((states // LANES, tl, LANES), F32), pltpu.VMEM((states // LANES, tl, LANES), F32),
                        pltpu.VMEM((1, states), F32), pltpu.VMEM((1, states), F32),
                        pltpu.VMEM((tl, width), F32)],
        compiler_params=_params(("arbitrary", "arbitrary"), 40),
        name="s5",
    )(proj, perm, unperm, wb, wc, a_re, a_im, pw_re, pw_im, d, gw, gb)


def _sgu_body(zu_ref, zv_ref, g_ref, b_ref, w_ref, bias_ref, o_ref, *, tl):
    width = zu_ref.shape[1]
    u = jax.nn.gelu(zu_ref[...])
    v = _layer_norm(jax.nn.gelu(zv_ref[...]), g_ref[...], b_ref[...])
    vb = v.astype(BF16)
    row = lax.broadcasted_iota(jnp.int32, (2 * SGU_CHUNK, SGU_CHUNK), 0)
    col = lax.broadcasted_iota(jnp.int32, (2 * SGU_CHUNK, SGU_CHUNK), 1)
    causal = col <= (row % SGU_CHUNK)
    lane = lax.broadcasted_iota(jnp.int32, (SGU_CHUNK, LANES), 1)
    first = lane < SGU_HEAD_DIM
    for j in range(width // LANES):
        w = jnp.where(causal, w_ref[j], 0.0).astype(BF16)
        bias = bias_ref[j]
        for c in range(tl // SGU_CHUNK):
            rs = slice(c * SGU_CHUNK, (c + 1) * SGU_CHUNK)
            cs = slice(j * LANES, (j + 1) * LANES)
            r = _dot(w, vb[rs, cs])
            mixed = jnp.where(first, r[:SGU_CHUNK], r[SGU_CHUNK:]) + bias
            o_ref[rs, cs] = (u[rs, cs] * mixed).astype(o_ref.dtype)


def _sgu(proj, n, width, norm_g, norm_b, w_s, b_s, tl=512):
    heads = w_s.shape[0]
    tl = min(tl, n)
    w_pairs = w_s.astype(F32).reshape(heads // 2, 2 * SGU_CHUNK, SGU_CHUNK)
    bias = jnp.repeat(b_s.astype(F32).T, SGU_HEAD_DIM, axis=1)
    bias = bias.reshape(SGU_CHUNK, heads // 2, LANES).transpose(1, 0, 2)
    g = norm_g.astype(F32).reshape(1, width)
    b = norm_b.astype(F32).reshape(1, width)
    full = lambda a: pl.BlockSpec(a.shape, lambda i: (0,) * a.ndim)
    return pl.pallas_call(
        functools.partial(_sgu_body, tl=tl),
        grid=(n // tl,),
        in_specs=[pl.BlockSpec((tl, width), lambda i: (i, 1)),
                  pl.BlockSpec((tl, width), lambda i: (i, 2)),
                  full(g), full(b), full(w_pairs), full(bias)],
        out_specs=pl.BlockSpec((tl, width), lambda i: (i, 0)),
        out_shape=jax.ShapeDtypeStruct((n, width), BF16),
        compiler_params=_params(("parallel",), 40),
        name="sgu",
    )(proj, proj, g, b, w_pairs, bias)


def _pool_body(p_ref, w_ref, sc_ref, o_ref, ext, *, tl):
    l = pl.program_id(1)

    @pl.when(l == 0)
    def _():
        ext[0:POOL_HALO, :] = jnp.zeros((POOL_HALO, ext.shape[1]), F32)

    @pl.when(l > 0)
    def _():
        ext[0:POOL_HALO, :] = ext[tl:tl + POOL_HALO, :]

    ext[POOL_HALO:POOL_HALO + tl, :] = p_ref[...]
    pos = l * tl + lax.broadcasted_iota(jnp.int32, (tl, LANES), 0)
    for gi, win in enumerate(POOL_WINDOWS):
        cs = slice(gi * LANES, (gi + 1) * LANES)
        x = ext[POOL_HALO:POOL_HALO + tl, cs]
        acc = ext[:, cs]
        span = 1
        while span < win:
            acc = acc + pltpu.roll(acc, span, 0)
            span *= 2
        acc = acc[POOL_HALO:, :]
        count = jnp.minimum(pos + 1, win).astype(F32)
        pooled = acc / count - x
        y = _dot(pooled.astype(BF16), w_ref[gi]) * sc_ref[:, cs]
        o_ref[:, cs] = y.astype(o_ref.dtype)


def _pool(proj, bsz, seq, width, w_pool, scale, col_block, tl=512):
    tl = min(tl, seq)
    nl = seq // tl
    w = w_pool.astype(BF16)
    sc = scale.astype(F32).reshape(1, width)
    return pl.pallas_call(
        functools.partial(_pool_body, tl=tl),
        grid=(bsz, nl),
        in_specs=[pl.BlockSpec((tl, width), lambda b, l: (b * nl + l, col_block)),
                  pl.BlockSpec(w.shape, lambda b, l: (0, 0, 0)),
                  pl.BlockSpec(sc.shape, lambda b, l: (0, 0))],
        out_specs=pl.BlockSpec((tl, width), lambda b, l: (b * nl + l, 0)),
        out_shape=jax.ShapeDtypeStruct((bsz * seq, width), BF16),
        scratch_shapes=[pltpu.VMEM((tl + POOL_HALO, width), F32)],
        compiler_params=_params(("arbitrary", "arbitrary"), 40),
        name="pool",
    )(proj, w, sc)


def _dn_prep_body(q_ref, k_ref, v_ref, tail_ref, cw_ref, alog_ref, dtb_ref,
                  vk_ref, qg_ref, kt_ref, qk_ref, low_ref, gl_ref, ext, *, tl, heads):
    l = pl.program_id(1)
    width = heads * DN_HEAD_DIM
    c = DN_CHUNK

    @pl.when(l == 0)
    def _():
        ext[:, 0:DN_HALO, :] = jnp.zeros((3, DN_HALO, width), F32)

    @pl.when(l > 0)
    def _():
        ext[:, 0:DN_HALO, :] = ext[:, tl:tl + DN_HALO, :]

    mixed = []
    for i, ref in enumerate((q_ref, k_ref, v_ref)):
        ext[i, DN_HALO:DN_HALO + tl, :] = ref[...]
        e = ext[i]
        acc = e * cw_ref[0:1, i * width:(i + 1) * width]
        for j in range(1, DN_CONV):
            acc = pltpu.roll(acc, 1, 0) + e * cw_ref[j:j + 1, i * width:(i + 1) * width]
        acc = acc[DN_HALO:, :]
        mixed.append(acc * jax.nn.sigmoid(acc))
    qc, kc, vc = mixed

    tail = tail_ref[...]
    sp_in = tail + dtb_ref[...]
    softplus = jnp.maximum(sp_in, 0.0) + jnp.log1p(jnp.exp(-jnp.abs(sp_in)))
    g_all = -jnp.exp(alog_ref[...]) * softplus
    beta_all = jax.nn.sigmoid(tail)

    ri = lax.broadcasted_iota(jnp.int32, (c, c), 0)
    ci = lax.broadcasted_iota(jnp.int32, (c, c), 1)
    causal = ri >= ci
    strict = ri > ci
    tri = causal.astype(BF16)
    sel = (lax.broadcasted_iota(jnp.int32, (heads * c, LANES), 0) // c
           == lax.broadcasted_iota(jnp.int32, (heads * c, LANES), 1)).astype(BF16)
    chunks = range(tl // c)

    gc = []
    for n in chunks:
        g1, g2, g3 = _split3(g_all[n * c:(n + 1) * c, :])
        gc.append(_dot(tri, g1) + _dot(tri, g2) + _dot(tri, g3))
    grow = []
    for n in chunks:
        g1, g2, g3 = _split3(gc[n])
        grow.append(_dot_nt(sel, g1) + _dot_nt(sel, g2) + _dot_nt(sel, g3))
    for n in chunks:
        rs = slice(n * c, (n + 1) * c)
        qn, kn, kbeta, kq, kkt = {}, {}, {}, {}, {}
        for h in range(heads):
            hs = slice(h * DN_HEAD_DIM, (h + 1) * DN_HEAD_DIM)
            q = qc[rs, hs]
            k = kc[rs, hs]
            q = q * lax.rsqrt(jnp.sum(q * q, axis=-1, keepdims=True) + L2_EPS) * (DN_HEAD_DIM ** -0.5)
            k = k * lax.rsqrt(jnp.sum(k * k, axis=-1, keepdims=True) + L2_EPS)
            kb = k * beta_all[rs, heads + h:heads + h + 1]
            qn[n, h], kn[n, h], kbeta[n, h] = q, k, kb
            if h % 2 == 0:
                kq[n, h] = _dot_nt(jnp.concatenate([kb.astype(BF16), q.astype(BF16)], axis=0), k.astype(BF16))
            else:
                kq[n, h] = _dot_nt(q.astype(BF16), k.astype(BF16))
                kkt[n, h] = _dot_nt(k.astype(BF16), kb.astype(BF16))
        for h in range(heads):
            hs = slice(h * DN_HEAD_DIM, (h + 1) * DN_HEAD_DIM)
            ms = slice(h * c, (h + 1) * c)
            gcol = gc[n][:, h:h + 1]
            glast = gc[n][c - 1:c, h:h + 1]
            decay = jnp.exp(-jnp.abs(gcol - grow[n][ms, :]))
            egc = jnp.exp(gcol)
            if h % 2 == 0:
                packed = jnp.where(strict, kq[n, h][:c] * decay, 0.0)
                qk_ref[rs, ms] = jnp.where(causal, kq[n, h][c:] * decay, 0.0)
            else:
                low_ref[rs, (h // 2) * c:(h // 2 + 1) * c] = jnp.where(ri < ci, kkt[n, h] * decay, packed)
                qk_ref[rs, ms] = jnp.where(causal, kq[n, h] * decay, 0.0)
            beta = beta_all[rs, heads + h:heads + h + 1]
            vk_ref[rs, 2 * h * DN_HEAD_DIM:(2 * h + 1) * DN_HEAD_DIM] = (vc[rs, hs] * beta).astype(BF16)
            vk_ref[rs, (2 * h + 1) * DN_HEAD_DIM:(2 * h + 2) * DN_HEAD_DIM] = (kbeta[n, h] * egc).astype(BF16)
            qg_ref[rs, hs] = (qn[n, h] * egc).astype(BF16)
            kt_ref[rs, hs] = (kn[n, h] * jnp.exp(glast - gcol)).astype(BF16)
            gl_ref[n * heads + h:n * heads + h + 1, :] = jnp.broadcast_to(jnp.exp(glast), (1, LANES))


def _tri_inv_body(l_ref, t_ref, u_scr):
    c = l_ref.shape[0]
    groups = c // SUBLANES
    sub = lax.broadcasted_iota(jnp.int32, (SUBLANES, LANES), 0)
    zero = jnp.zeros((SUBLANES, LANES), F32)
    grp = lambda cg: slice(cg * SUBLANES, (cg + 1) * SUBLANES)

    for i in range(c):
        diag = i // SUBLANES
        acc = [zero] * (diag + 1)
        acc[diag] = (sub == i % SUBLANES).astype(F32)
        for j in range(i):
            lij = jnp.broadcast_to(l_ref[i, j:j + 1, :], (SUBLANES, LANES))
            for cg in range(j // SUBLANES + 1):
                acc[cg] = acc[cg] - lij * t_ref[j, grp(cg), :]
        for cg in range(diag + 1):
            t_ref[i, grp(cg), :] = acc[cg]

    for i in reversed(range(c)):
        diag = i // SUBLANES
        acc = {cg: zero for cg in range(diag, groups)}
        acc[diag] = (sub == i % SUBLANES).astype(F32)
        for j in range(i + 1, c):
            lij = jnp.broadcast_to(l_ref[i, j:j + 1, :], (SUBLANES, LANES))
            for cg in range(j // SUBLANES, groups):
                acc[cg] = acc[cg] - lij * u_scr[j, grp(cg), :]
        for cg in range(diag, groups):
            u_scr[i, grp(cg), :] = acc[cg]

    for i in range(c):
        diag = i // SUBLANES
        for cg in range(diag + 1, groups):
            t_ref[i, grp(cg), :] = u_scr[i, grp(cg), :]
        t_ref[i, grp(diag), :] = jnp.where(sub < i % SUBLANES, t_ref[i, grp(diag), :],
                                           jnp.where(sub > i % SUBLANES, u_scr[i, grp(diag), :], 0.0))


def _dn_rec_body(t_ref, vk_ref, qg_ref, kt_ref, qk_ref, gl_ref, gate_ref, ng_ref, o_ref, state,
                 *, tl, heads):
    c = DN_CHUNK
    dk = DN_HEAD_DIM

    @pl.when(pl.program_id(1) == 0)
    def _():
        state[...] = jnp.zeros_like(state)

    chunks = range(tl // c)
    rows = lambda n: slice(n * c, (n + 1) * c)
    hcol = lambda h: slice(h * dk, (h + 1) * dk)
    mcol = lambda h: slice(h * c, (h + 1) * c)
    ri = lax.broadcasted_iota(jnp.int32, (c, c), 0)
    ci = lax.broadcasted_iota(jnp.int32, (c, c), 1)

    uw, uwb, kuw = {}, {}, {}
    for n in chunks:
        for h in range(heads):
            packed = t_ref[rows(n), mcol(h // 2)]
            vk = vk_ref[rows(n), 2 * h * dk:2 * (h + 1) * dk]
            if h % 2 == 0:
                uw[n, h] = _dot(jnp.where(ri > ci, packed, 0.0).astype(BF16), vk) + vk.astype(F32)
            else:
                uw[n, h] = _dot_tn(jnp.where(ri < ci, packed, 0.0).astype(BF16), vk) + vk.astype(F32)
            uwb[n, h] = uw[n, h].astype(BF16)
    for n in chunks:
        for h in range(heads):
            kuw[n, h] = _dot_tn(kt_ref[rows(n), hcol(h)], uwb[n, h])

    def emit(n, h, wq, v_bf16):
        o = wq[c:] + _dot(qk_ref[rows(n), mcol(h)].astype(BF16), v_bf16)
        o = o * lax.rsqrt(jnp.mean(o * o, axis=-1, keepdims=True) + RMS_EPS) * ng_ref[...]
        gt = gate_ref[rows(n), hcol(h)]
        o_ref[rows(n), hcol(h)] = (o * (gt * jax.nn.sigmoid(gt))).astype(o_ref.dtype)

    s = [state[h] for h in range(heads)]
    pending = []
    for n in chunks:
        sb = [s[h].astype(BF16) for h in range(heads)]
        for h in range(heads):
            s[h] = (s[h] * gl_ref[n * heads + h:n * heads + h + 1, :] + kuw[n, h][:, :dk]
                    - _dot(kuw[n, h][:, dk:].astype(BF16), sb[h]))
        wq = []
        for h in range(heads):
            lhs = jnp.concatenate([uwb[n, h][:, dk:], qg_ref[rows(n), hcol(h)]], axis=0)
            wq.append(_dot(lhs, sb[h]))
        for args in pending:
            emit(*args)
        pending = [(n, h, wq[h], (uw[n, h][:, :dk] - wq[h][:c]).astype(BF16)) for h in range(heads)]
    for args in pending:
        emit(*args)
    for h in range(heads):
        state[h] = s[h]


def _dn(proj, tail, bsz, seq, width, col_block, conv_w, a_log, dt_bias, norm_g, tl=256):
    heads = width // DN_HEAD_DIM
    c = DN_CHUNK
    n = bsz * seq
    tl = min(tl, seq)
    nl = seq // tl
    chunks = n // c
    cw = conv_w.astype(F32)
    pad = lambda a: jnp.pad(a.astype(F32), (0, LANES - heads)).reshape(1, LANES)
    alog = pad(a_log)
    dtb = pad(dt_bias)
    ng = norm_g.astype(F32).reshape(1, DN_HEAD_DIM)
    full = lambda a: pl.BlockSpec(a.shape, lambda b, l: (0,) * a.ndim)
    slab = lambda k: pl.BlockSpec((tl, width), lambda b, l: (b * nl + l, col_block + k))
    rows = lambda wd: pl.BlockSpec((tl, wd), lambda b, l: (b * nl + l, 0))
    gl_spec = pl.BlockSpec((tl // c * heads, LANES), lambda b, l: (b * nl + l, 0))
    wide = jax.ShapeDtypeStruct((n, width), BF16)
    mats = jax.ShapeDtypeStruct((n, heads * c), F32)
    pairs = heads // 2
    packed = jax.ShapeDtypeStruct((n, pairs * c), F32)
    vk, qg, kt, qk, low, gl = pl.pallas_call(
        functools.partial(_dn_prep_body, tl=tl, heads=heads),
        grid=(bsz, nl),
        in_specs=[slab(0), slab(1), slab(2), rows(LANES), full(cw), full(alog), full(dtb)],
        out_specs=[rows(2 * width)] + [rows(width)] * 2 + [rows(heads * c), rows(pairs * c), gl_spec],
        out_shape=[jax.ShapeDtypeStruct((n, 2 * width), BF16)] + [wide] * 2 + [mats, packed]
        + [jax.ShapeDtypeStruct((chunks * heads, LANES), F32)],
        scratch_shapes=[pltpu.VMEM((3, tl + DN_HALO, width), F32)],
        compiler_params=_params(("arbitrary", "arbitrary"), 40),
        name="deltanet_prep",
    )(proj, proj, proj, tail, cw, alog, dtb)

    m_total = chunks * pairs
    low_t = low.reshape(chunks, c, pairs, c).transpose(1, 3, 0, 2).reshape(c, c, m_total)
    t_t = pl.pallas_call(
        _tri_inv_body,
        grid=(m_total // LANES,),
        in_specs=[pl.BlockSpec((c, c, LANES), lambda i: (0, 0, i))],
        out_specs=pl.BlockSpec((c, c, LANES), lambda i: (0, 0, i)),
        out_shape=jax.ShapeDtypeStruct((c, c, m_total), F32),
        scratch_shapes=[pltpu.VMEM((c, c, LANES), F32)],
        compiler_params=_params(("parallel",), 40),
        name="deltanet_tri_inv",
    )(low_t)
    t_mat = t_t.reshape(c, c, chunks, pairs).transpose(2, 0, 3, 1).reshape(n, pairs * c)

    return pl.pallas_call(
        functools.partial(_dn_rec_body, tl=tl, heads=heads),
        grid=(bsz, nl),
        in_specs=[rows(pairs * c), rows(2 * width)] + [rows(width)] * 2
        + [rows(heads * c), gl_spec, slab(3), full(ng)],
        out_specs=rows(width),
        out_shape=wide,
        scratch_shapes=[pltpu.VMEM((heads, DN_HEAD_DIM, DN_HEAD_DIM), F32)],
        compiler_params=_params(("arbitrary", "arbitrary"), 40),
        name="deltanet_rec",
    )(t_mat, vk, qg, kt, qk, gl, proj, ng)


def _out_body(y0_ref, y1_ref, y2_ref, y3_ref, w_ref, x_ref, g_ref, b_ref, o_ref, ob_ref, *, alpha):
    width = y0_ref.shape[1]
    half = x_ref.shape[0] // 2
    accs = []
    for rows in (slice(0, half), slice(half, 2 * half)):
        mixed = jnp.concatenate([ref[rows, :] for ref in (y0_ref, y1_ref, y2_ref, y3_ref)], axis=1)
        accs.append((rows, alpha * x_ref[rows, :] + _dot(mixed, w_ref[...])))
    for rows, acc in accs:
        y = _layer_norm(acc, g_ref[...], b_ref[...])
        o_ref[rows, :] = y
        ob_ref[rows, :] = y.astype(BF16)


def _out_proj(ys, w_out, x, g, b, alpha, tm=512):
    n, d = x.shape
    width = ys[0].shape[1]
    tm = min(tm, n)
    row = lambda wd: pl.BlockSpec((tm, wd), lambda i: (i, 0))
    full = lambda a: pl.BlockSpec(a.shape, lambda i: (0,) * a.ndim)
    g = g.astype(F32).reshape(1, d)
    b = b.astype(F32).reshape(1, d)
    return pl.pallas_call(
        functools.partial(_out_body, alpha=alpha),
        grid=(n // tm,),
        in_specs=[row(width)] * 4 + [full(w_out), row(d), full(g), full(b)],
        out_specs=[row(d), row(d)],
        out_shape=[jax.ShapeDtypeStruct((n, d), F32), jax.ShapeDtypeStruct((n, d), BF16)],
        compiler_params=_params(("parallel",), 56),
        name="out_proj_ln",
    )(*ys, w_out, x, g, b)


def _ffn_body(xb_ref, wu_ref, wd_ref, x_ref, g_ref, b_ref, o_ref, ob_ref, acc, *, alpha):
    f = pl.program_id(1)

    @pl.when(f == 0)
    def _():
        acc[...] = alpha * x_ref[...]

    hidden = jnp.maximum(_dot(xb_ref[...], wu_ref[...]), 0.0)
    acc[...] += _dot((hidden * hidden).astype(BF16), wd_ref[...])

    @pl.when(f == pl.num_programs(1) - 1)
    def _():
        y = _layer_norm(acc[...], g_ref[...], b_ref[...])
        o_ref[...] = y
        ob_ref[...] = y.astype(BF16)


def _ffn(xb, x, w_up, w_down, g, b, alpha, tm=512, tf=1024):
    n, d = x.shape
    ff = w_up.shape[1]
    tm = min(tm, n)
    g = g.astype(F32).reshape(1, d)
    b = b.astype(F32).reshape(1, d)
    row = pl.BlockSpec((tm, d), lambda i, f: (i, 0))
    vec = pl.BlockSpec((1, d), lambda i, f: (0, 0))
    return pl.pallas_call(
        functools.partial(_ffn_body, alpha=alpha),
        grid=(n // tm, ff // tf),
        in_specs=[row, pl.BlockSpec((d, tf), lambda i, f: (0, f)), pl.BlockSpec((tf, d), lambda i, f: (f, 0)),
                  row, vec, vec],
        out_specs=[row, row],
        out_shape=[jax.ShapeDtypeStruct((n, d), F32), jax.ShapeDtypeStruct((n, d), BF16)],
        scratch_shapes=[pltpu.VMEM((tm, d), F32)],
        compiler_params=_params(("parallel", "arbitrary"), 56),
        name="ffn_ln",
    )(xb, w_up, w_down, x, g, b)


def kernel(x, w_in, s5_lambda_re, s5_lambda_im, s5_log_step, s5_b_re, s5_b_im, s5_c_re, s5_c_im, s5_d, s5_glu_w, s5_glu_b, sgu_norm_g, sgu_norm_b, sgu_w, sgu_b, pool_w, pool_scale, dn_conv_w, dn_a_log, dn_dt_bias, dn_norm_g, w_out, ln1_g, ln1_b, w_up, w_down, ln2_g, ln2_b):
    bsz, seq, d = x.shape
    depth = w_in.shape[0]
    n = bsz * seq
    width = s5_glu_w.shape[1]
    heads = dn_a_log.shape[1]
    main_cols = 8 * width
    alpha = (2 * depth) ** 0.25
    s5_tl = min(256, seq)

    xf = x.reshape(n, d).astype(F32)
    xb = xf
    w_in_t = jnp.swapaxes(w_in, 1, 2)
    for i in range(depth):
        proj, tail = _proj(xb, _cast_bf16(w_in_t, i, 3 * LANES, pad_rows_to=main_cols + LANES), main_cols)
        prep = _s5_prepare(s5_lambda_re[i], s5_lambda_im[i], s5_log_step[i], s5_b_re[i], s5_b_im[i],
                           s5_c_re[i], s5_c_im[i], s5_d[i], s5_tl // SUBLANES)
        y_s5 = _s5(proj, bsz, seq, prep, s5_glu_w[i], s5_glu_b[i], tl=s5_tl)
        y_sgu = _sgu(proj, n, width, sgu_norm_g[i], sgu_norm_b[i], sgu_w[i], sgu_b[i])
        y_pool = _pool(proj, bsz, seq, width, pool_w[i], pool_scale[i], col_block=3)
        y_dn = _dn(proj, tail, bsz, seq, width, 4, dn_conv_w[i], dn_a_log[i], dn_dt_bias[i], dn_norm_g[i])
        xf, xb = _out_proj((y_s5, y_sgu, y_pool, y_dn), _cast_bf16(w_out, i, 1024), xf, ln1_g[i], ln1_b[i], alpha)
        xf, xb = _ffn(xb, xf, _cast_bf16(w_up, i, 256), _cast_bf16(w_down, i, 1024), ln2_g[i], ln2_b[i], alpha)
    return xf.reshape(bsz, seq, d).astype(x.dtype)
```

```python
import functools
import math

import jax
import jax.numpy as jnp
from jax import lax
from jax.experimental import pallas as pl
from jax.experimental.pallas import tpu as pltpu

F32 = jnp.float32
BF16 = jnp.bfloat16
HIGHEST = lax.Precision.HIGHEST

LANES = 128
SUBLANES = 8

S5_CH_PER_GROUP = 16
S5_STATE = 64
S5_GROUPS_PER_SLAB = LANES // S5_CH_PER_GROUP
S5_SLAB_STATE = S5_GROUPS_PER_SLAB * S5_STATE
SGU_CHUNK = 128
SGU_HEAD_DIM = 64
POOL_WINDOWS = (2, 4, 8, 16)
POOL_HALO = 16
DN_HEAD_DIM = 128
DN_CONV = 4
DN_CHUNK = 64
DN_HALO = 8
LN_EPS = 1e-5
RMS_EPS = 1e-6
L2_EPS = 1e-6


def _params(semantics, vmem_mib):
    return pltpu.CompilerParams(dimension_semantics=semantics, vmem_limit_bytes=vmem_mib * 1024 * 1024)


def _layer_norm(y, g, b):
    mu = jnp.mean(y, axis=-1, keepdims=True)
    yc = y - mu
    var = jnp.mean(yc * yc, axis=-1, keepdims=True)
    return yc * lax.rsqrt(var + LN_EPS) * g + b


def _dot(a, b):
    return jnp.dot(a, b, preferred_element_type=F32)


def _dot_nt(a, b, precision=None):
    return lax.dot_general(a, b, (((1,), (1,)), ((), ())), precision=precision, preferred_element_type=F32)


def _dot_tn(a, b):
    return lax.dot_general(a, b, (((0,), (0,)), ((), ())), preferred_element_type=F32)


def _split3(x):
    hi = x.astype(BF16)
    rest = x - hi.astype(F32)
    mid = rest.astype(BF16)
    lo = (rest - mid.astype(F32)).astype(BF16)
    return hi, mid, lo


def _proj_body(x_ref, w_ref, wt_ref, o_ref, ot_ref, *scratch):
    if scratch:
        xb_ref, = scratch

        @pl.when(pl.program_id(1) == 0)
        def _():
            xb_ref[...] = x_ref[...].astype(BF16)
    else:
        xb_ref = x_ref
    o_ref[...] = _dot_nt(xb_ref[...], w_ref[...])

    @pl.when(pl.program_id(1) == 0)
    def _():
        ot_ref[...] = _dot_nt(xb_ref[...], wt_ref[...])


def _proj(x, w_t, cols, tm=1024, tn=1024):
    n, d = x.shape
    tm = min(tm, n)
    scratch = [] if x.dtype == BF16 else [pltpu.VMEM((tm, d), BF16)]
    return pl.pallas_call(
        _proj_body,
        grid=(n // tm, cols // tn),
        in_specs=[pl.BlockSpec((tm, d), lambda i, j: (i, 0)),
                  pl.BlockSpec((tn, d), lambda i, j: (j, 0)),
                  pl.BlockSpec((LANES, d), lambda i, j: (cols // LANES, 0))],
        out_specs=[pl.BlockSpec((tm, tn), lambda i, j: (i, j)),
                   pl.BlockSpec((tm, LANES), lambda i, j: (i, 0))],
        out_shape=[jax.ShapeDtypeStruct((n, cols), F32), jax.ShapeDtypeStruct((n, LANES), F32)],
        scratch_shapes=scratch,
        compiler_params=_params(("parallel", "arbitrary"), 56),
        name="proj",
    )(x, w_t, w_t)


def _cast_body(w_ref, o_ref, *, valid_rows):
    w = w_ref[...]
    if valid_rows % w.shape[0]:
        row = pl.program_id(0) * w.shape[0] + lax.broadcasted_iota(jnp.int32, w.shape, 0)
        w = jnp.where(row < valid_rows, w, 0.0)
    o_ref[...] = w.astype(o_ref.dtype)


def _cast_bf16(w, layer, block_rows, pad_rows_to=None):
    _, r, c = w.shape
    out_r = pad_rows_to or r
    return pl.pallas_call(
        functools.partial(_cast_body, valid_rows=r),
        grid=(out_r // block_rows,),
        in_specs=[pl.BlockSpec((None, block_rows, c), lambda i: (layer, i, 0))],
        out_specs=pl.BlockSpec((block_rows, c), lambda i: (i, 0)),
        out_shape=jax.ShapeDtypeStruct((out_r, c), BF16),
        compiler_params=_params(("parallel",), 40),
        name="cast_bf16",
    )(w)


def _s5_body(u_ref, perm_ref, unperm_ref, wb_ref, wc_ref, are_ref, aim_ref, pre_ref, pim_ref, d_ref, gw_ref, gb_ref,
             o_ref, sre, sim, hre, him, y_scr, *, sub, steps):
    n_slabs = u_ref.shape[1] // LANES
    n_sub = u_ref.shape[0] // sub
    sw = S5_SLAB_STATE
    tiles = sw // LANES

    @pl.when(pl.program_id(1) == 0)
    def _():
        hre[...] = jnp.zeros_like(hre)
        him[...] = jnp.zeros_like(him)

    perm = perm_ref[...]
    us = []
    for a in range(n_sub):
        u1, u2, u3 = _split3(u_ref[a * sub:(a + 1) * sub, :])
        u = _dot(perm, u1) + _dot(perm, u2) + _dot(perm, u3)
        us.append(u)
        ub = u.astype(BF16)
        for j in range(n_slabs):
            bu = _dot(ub[:, j * LANES:(j + 1) * LANES], wb_ref[j])
            for k in range(tiles):
                sre[a, j * tiles + k] = bu[:, k * LANES:(k + 1) * LANES]
                sim[a, j * tiles + k] = bu[:, sw + k * LANES:sw + (k + 1) * LANES]

    for a in range(n_sub):
        out_rows = slice(a * sub, (a + 1) * sub)
        for j in range(n_slabs):
            ts = [j * tiles + k for k in range(tiles)]
            cols = [slice(t * LANES, (t + 1) * LANES) for t in ts]

            ar = [jnp.broadcast_to(are_ref[:, c], (SUBLANES, LANES)) for c in cols]
            ai = [jnp.broadcast_to(aim_ref[:, c], (SUBLANES, LANES)) for c in cols]
            hr = [jnp.zeros((SUBLANES, LANES), F32)] * tiles
            hi = [jnp.zeros((SUBLANES, LANES), F32)] * tiles
            for r in range(steps):
                rows = slice(r * SUBLANES, (r + 1) * SUBLANES)
                for k, t in enumerate(ts):
                    hr[k], hi[k] = (ar[k] * hr[k] - ai[k] * hi[k] + sre[a, t, rows, :],
                                    ar[k] * hi[k] + ai[k] * hr[k] + sim[a, t, rows, :])
                    sre[a, t, rows, :] = hr[k]
                    sim[a, t, rows, :] = hi[k]

            cr8, ci8 = [], []
            for k in range(tiles):
                apr = pre_ref[steps - 1:steps, cols[k]]
                api = pim_ref[steps - 1:steps, cols[k]]
                cr = hre[:, cols[k]]
                ci = him[:, cols[k]]
                crs, cis = [], []
                for s in range(SUBLANES):
                    crs.append(cr)
                    cis.append(ci)
                    cr, ci = (apr * cr - api * ci + hr[k][s:s + 1, :], apr * ci + api * cr + hi[k][s:s + 1, :])
                hre[:, cols[k]] = cr
                him[:, cols[k]] = ci
                cr8.append(jnp.concatenate(crs, axis=0))
                ci8.append(jnp.concatenate(cis, axis=0))
            for r in range(steps):
                rows = slice(r * SUBLANES, (r + 1) * SUBLANES)
                for k, t in enumerate(ts):
                    pr = jnp.broadcast_to(pre_ref[r:r + 1, cols[k]], (SUBLANES, LANES))
                    pi = jnp.broadcast_to(pim_ref[r:r + 1, cols[k]], (SUBLANES, LANES))
                    sre[a, t, rows, :] = sre[a, t, rows, :] + (pr * cr8[k] - pi * ci8[k])
                    sim[a, t, rows, :] = sim[a, t, rows, :] + (pr * ci8[k] + pi * cr8[k])

            h_re = jnp.concatenate([sre[a, t] for t in ts], axis=1).astype(BF16)
            h_im = jnp.concatenate([sim[a, t] for t in ts], axis=1).astype(BF16)
            y = _dot(h_re, wc_ref[j, :sw, :]) + _dot(h_im, wc_ref[j, sw:, :])
            lc = slice(j * LANES, (j + 1) * LANES)
            y = y + d_ref[:, lc] * us[a][:, lc]
            y_scr[out_rows, lc] = jax.nn.gelu(y)

        y = y_scr[out_rows, :]
        gate = _dot(y.astype(BF16), gw_ref[...]) + gb_ref[...]
        out = (y * jax.nn.sigmoid(gate)).astype(BF16)
        o_ref[out_rows, :] = _dot(unperm_ref[...], out).astype(o_ref.dtype)


def _s5_prepare(lam_re, lam_im, log_step, b_re, b_im, c_re, c_im, d, steps):
    g, p = lam_re.shape
    h = S5_CH_PER_GROUP
    q = S5_GROUPS_PER_SLAB
    n_slabs = g // q
    step = jnp.exp(log_step.astype(F32))[:, None]
    lr, li = lam_re.astype(F32), lam_im.astype(F32)
    mag = jnp.exp(lr * step)
    a_re, a_im = mag * jnp.cos(li * step), mag * jnp.sin(li * step)
    den = lr * lr + li * li
    f_re = ((a_re - 1.0) * lr + a_im * li) / den
    f_im = (a_im * lr - (a_re - 1.0) * li) / den
    bb_re = f_re[:, :, None] * b_re - f_im[:, :, None] * b_im
    bb_im = f_re[:, :, None] * b_im + f_im[:, :, None] * b_re
    eye = jnp.eye(q, dtype=F32)

    def in_block(m):
        m = m.reshape(n_slabs, q, p, h)
        return jnp.einsum('jqph,qr->jqhrp', m, eye).reshape(n_slabs, q * h, q * p)

    def out_block(m):
        m = m.reshape(n_slabs, q, h, p)
        return jnp.einsum('jqhp,qr->jqprh', m, eye).reshape(n_slabs, q * p, q * h)

    wb = jnp.concatenate([in_block(bb_re), in_block(bb_im)], axis=2).astype(BF16)
    wc = jnp.concatenate([out_block(c_re.astype(F32)), out_block(-c_im.astype(F32))], axis=1).astype(BF16)
    k = jnp.arange(1, steps + 1, dtype=F32)[:, None, None]
    pmag = jnp.exp(lr * step * k)
    ang = li * step * k
    pw_re = (pmag * jnp.cos(ang)).reshape(steps, g * p)
    pw_im = (pmag * jnp.sin(ang)).reshape(steps, g * p)
    return (wb, wc, a_re.reshape(1, g * p), a_im.reshape(1, g * p), pw_re, pw_im,
            d.astype(F32).reshape(1, g * h))


def _s5(proj, bsz, seq, prep, glu_w, glu_b, sub, n_sub=2):
    wb, wc, a_re, a_im, pw_re, pw_im, d = prep
    tl = min(sub * n_sub, seq)
    steps = sub // SUBLANES
    width = d.shape[1]
    states = pw_re.shape[1]
    nl = seq // tl
    full = lambda a: pl.BlockSpec(a.shape, lambda b, l: (0,) * a.ndim)
    gw = glu_w.astype(BF16)
    gb = glu_b.astype(F32).reshape(1, width)
    dest = jnp.arange(sub)
    src = (dest % SUBLANES) * steps + dest // SUBLANES
    perm = (src[:, None] == jnp.arange(sub)[None, :]).astype(BF16)
    unperm = perm.T
    state_scratch = pltpu.VMEM((tl // sub, states // LANES, sub, LANES), F32)
    return pl.pallas_call(
        functools.partial(_s5_body, sub=sub, steps=steps),
        grid=(bsz, nl),
        in_specs=[pl.BlockSpec((tl, width), lambda b, l: (b * nl + l, 0)), full(perm), full(unperm),
                  full(wb), full(wc), full(a_re), full(a_im), full(pw_re), full(pw_im), full(d), full(gw), full(gb)],
        out_specs=pl.BlockSpec((tl, width), lambda b, l: (b * nl + l, 0)),
        out_shape=jax.ShapeDtypeStruct((bsz * seq, width), BF16),
        scratch_shapes=[state_scratch, state_scratch,
                        pltpu.VMEM((1, states), F32), pltpu.VMEM((1, states), F32),
                        pltpu.VMEM((tl, width), F32)],
        compiler_params=_params(("arbitrary", "arbitrary"), 40),
        name="s5",
    )(proj, perm, unperm, wb, wc, a_re, a_im, pw_re, pw_im, d, gw, gb)


def _sgu_body(zu_ref, zv_ref, g_ref, b_ref, w_ref, bias_ref, o_ref, *, tl):
    width = zu_ref.shape[1]
    u = jax.nn.gelu(zu_ref[...])
    v = _layer_norm(jax.nn.gelu(zv_ref[...]), g_ref[...], b_ref[...])
    vb = v.astype(BF16)
    row = lax.broadcasted_iota(jnp.int32, (2 * SGU_CHUNK, SGU_CHUNK), 0)
    col = lax.broadcasted_iota(jnp.int32, (2 * SGU_CHUNK, SGU_CHUNK), 1)
    causal = col <= (row % SGU_CHUNK)
    lane = lax.broadcasted_iota(jnp.int32, (SGU_CHUNK, LANES), 1)
    first = lane < SGU_HEAD_DIM
    for j in range(width // LANES):
        w = jnp.where(causal, w_ref[j], 0.0).astype(BF16)
        bias = bias_ref[j]
        for c in range(tl // SGU_CHUNK):
            rs = slice(c * SGU_CHUNK, (c + 1) * SGU_CHUNK)
            cs = slice(j * LANES, (j + 1) * LANES)
            r = _dot(w, vb[rs, cs])
            mixed = jnp.where(first, r[:SGU_CHUNK], r[SGU_CHUNK:]) + bias
            o_ref[rs, cs] = (u[rs, cs] * mixed).astype(o_ref.dtype)


def _sgu(proj, n, width, norm_g, norm_b, w_s, b_s, tl=512):
    heads = w_s.shape[0]
    tl = min(tl, n)
    w_pairs = w_s.astype(F32).reshape(heads // 2, 2 * SGU_CHUNK, SGU_CHUNK)
    bias = jnp.repeat(b_s.astype(F32).T, SGU_HEAD_DIM, axis=1)
    bias = bias.reshape(SGU_CHUNK, heads // 2, LANES).transpose(1, 0, 2)
    g = norm_g.astype(F32).reshape(1, width)
    b = norm_b.astype(F32).reshape(1, width)
    full = lambda a: pl.BlockSpec(a.shape, lambda i: (0,) * a.ndim)
    return pl.pallas_call(
        functools.partial(_sgu_body, tl=tl),
        grid=(n // tl,),
        in_specs=[pl.BlockSpec((tl, width), lambda i: (i, 1)),
                  pl.BlockSpec((tl, width), lambda i: (i, 2)),
                  full(g), full(b), full(w_pairs), full(bias)],
        out_specs=pl.BlockSpec((tl, width), lambda i: (i, 0)),
        out_shape=jax.ShapeDtypeStruct((n, width), BF16),
        compiler_params=_params(("parallel",), 40),
        name="sgu",
    )(proj, proj, g, b, w_pairs, bias)


def _pool_body(p_ref, w_ref, sc_ref, o_ref, ext, *, tl):
    l = pl.program_id(1)

    @pl.when(l == 0)
    def _():
        ext[0:POOL_HALO, :] = jnp.zeros((POOL_HALO, ext.shape[1]), F32)

    @pl.when(l > 0)
    def _():
        ext[0:POOL_HALO, :] = ext[tl:tl + POOL_HALO, :]

    ext[POOL_HALO:POOL_HALO + tl, :] = p_ref[...]
    pos = l * tl + lax.broadcasted_iota(jnp.int32, (tl, LANES), 0)
    for gi, win in enumerate(POOL_WINDOWS):
        cs = slice(gi * LANES, (gi + 1) * LANES)
        x = ext[POOL_HALO:POOL_HALO + tl, cs]
        acc = ext[:, cs]
        span = 1
        while span < win:
            acc = acc + pltpu.roll(acc, span, 0)
            span *= 2
        acc = acc[POOL_HALO:, :]
        count = jnp.minimum(pos + 1, win).astype(F32)
        pooled = acc / count - x
        y = _dot(pooled.astype(BF16), w_ref[gi]) * sc_ref[:, cs]
        o_ref[:, cs] = y.astype(o_ref.dtype)


def _pool(proj, bsz, seq, width, w_pool, scale, col_block, tl=512):
    tl = min(tl, seq)
    nl = seq // tl
    w = w_pool.astype(BF16)
    sc = scale.astype(F32).reshape(1, width)
    return pl.pallas_call(
        functools.partial(_pool_body, tl=tl),
        grid=(bsz, nl),
        in_specs=[pl.BlockSpec((tl, width), lambda b, l: (b * nl + l, col_block)),
                  pl.BlockSpec(w.shape, lambda b, l: (0, 0, 0)),
                  pl.BlockSpec(sc.shape, lambda b, l: (0, 0))],
        out_specs=pl.BlockSpec((tl, width), lambda b, l: (b * nl + l, 0)),
        out_shape=jax.ShapeDtypeStruct((bsz * seq, width), BF16),
        scratch_shapes=[pltpu.VMEM((tl + POOL_HALO, width), F32)],
        compiler_params=_params(("arbitrary", "arbitrary"), 40),
        name="pool",
    )(proj, w, sc)


def _dn_prep_body(q_ref, k_ref, v_ref, tail_ref, cw_ref, alog_ref, dtb_ref,
                  vk_ref, qg_ref, kt_ref, qk_ref, low_ref, gl_ref, ext, *, tl, heads):
    l = pl.program_id(1)
    width = heads * DN_HEAD_DIM
    c = DN_CHUNK

    @pl.when(l == 0)
    def _():
        ext[:, 0:DN_HALO, :] = jnp.zeros((3, DN_HALO, width), F32)

    @pl.when(l > 0)
    def _():
        ext[:, 0:DN_HALO, :] = ext[:, tl:tl + DN_HALO, :]

    mixed = []
    for i, ref in enumerate((q_ref, k_ref, v_ref)):
        ext[i, DN_HALO:DN_HALO + tl, :] = ref[...]
        e = ext[i]
        acc = e * cw_ref[0:1, i * width:(i + 1) * width]
        for j in range(1, DN_CONV):
            acc = pltpu.roll(acc, 1, 0) + e * cw_ref[j:j + 1, i * width:(i + 1) * width]
        acc = acc[DN_HALO:, :]
        mixed.append(acc * jax.nn.sigmoid(acc))
    qc, kc, vc = mixed

    tail = tail_ref[...]
    sp_in = tail + dtb_ref[...]
    softplus = jnp.maximum(sp_in, 0.0) + jnp.log1p(jnp.exp(-jnp.abs(sp_in)))
    g_all = -jnp.exp(alog_ref[...]) * softplus
    beta_all = jax.nn.sigmoid(tail)

    ri = lax.broadcasted_iota(jnp.int32, (c, c), 0)
    ci = lax.broadcasted_iota(jnp.int32, (c, c), 1)
    causal = ri >= ci
    strict = ri > ci
    tri = causal.astype(BF16)
    sel = (lax.broadcasted_iota(jnp.int32, (heads * c, LANES), 0) // c
           == lax.broadcasted_iota(jnp.int32, (heads * c, LANES), 1)).astype(BF16)
    chunks = range(tl // c)

    gc = []
    for n in chunks:
        g1, g2, g3 = _split3(g_all[n * c:(n + 1) * c, :])
        gc.append(_dot(tri, g1) + _dot(tri, g2) + _dot(tri, g3))
    grow = []
    for n in chunks:
        g1, g2, g3 = _split3(gc[n])
        grow.append(_dot_nt(sel, g1) + _dot_nt(sel, g2) + _dot_nt(sel, g3))
    for n in chunks:
        rs = slice(n * c, (n + 1) * c)
        qn, kn, kbeta, kq, kkt = {}, {}, {}, {}, {}
        for h in range(heads):
            hs = slice(h * DN_HEAD_DIM, (h + 1) * DN_HEAD_DIM)
            q = qc[rs, hs]
            k = kc[rs, hs]
            q = q * lax.rsqrt(jnp.sum(q * q, axis=-1, keepdims=True) + L2_EPS) * (DN_HEAD_DIM ** -0.5)
            k = k * lax.rsqrt(jnp.sum(k * k, axis=-1, keepdims=True) + L2_EPS)
            kb = k * beta_all[rs, heads + h:heads + h + 1]
            qn[n, h], kn[n, h], kbeta[n, h] = q, k, kb
            if h % 2 == 0:
                kq[n, h] = _dot_nt(jnp.concatenate([kb.astype(BF16), q.astype(BF16)], axis=0), k.astype(BF16))
            else:
                kq[n, h] = _dot_nt(q.astype(BF16), k.astype(BF16))
                kkt[n, h] = _dot_nt(k.astype(BF16), kb.astype(BF16))
        for h in range(heads):
            hs = slice(h * DN_HEAD_DIM, (h + 1) * DN_HEAD_DIM)
            ms = slice(h * c, (h + 1) * c)
            gcol = gc[n][:, h:h + 1]
            glast = gc[n][c - 1:c, h:h + 1]
            decay = jnp.exp(-jnp.abs(gcol - grow[n][ms, :]))
            egc = jnp.exp(gcol)
            if h % 2 == 0:
                packed = jnp.where(strict, kq[n, h][:c] * decay, 0.0)
                qk_ref[rs, ms] = jnp.where(causal, kq[n, h][c:] * decay, 0.0)
            else:
                low_ref[rs, (h // 2) * c:(h // 2 + 1) * c] = jnp.where(ri < ci, kkt[n, h] * decay, packed)
                qk_ref[rs, ms] = jnp.where(causal, kq[n, h] * decay, 0.0)
            beta = beta_all[rs, heads + h:heads + h + 1]
            vk_ref[rs, 2 * h * DN_HEAD_DIM:(2 * h + 1) * DN_HEAD_DIM] = (vc[rs, hs] * beta).astype(BF16)
            vk_ref[rs, (2 * h + 1) * DN_HEAD_DIM:(2 * h + 2) * DN_HEAD_DIM] = (kbeta[n, h] * egc).astype(BF16)
            qg_ref[rs, hs] = (qn[n, h] * egc).astype(BF16)
            kt_ref[rs, hs] = (kn[n, h] * jnp.exp(glast - gcol)).astype(BF16)
            gl_ref[n * heads + h:n * heads + h + 1, :] = jnp.broadcast_to(jnp.exp(glast), (1, LANES))


def _tri_inv_body(l_ref, t_ref, u_scr):
    c = l_ref.shape[0]
    groups = c // SUBLANES
    sub = lax.broadcasted_iota(jnp.int32, (SUBLANES, LANES), 0)
    zero = jnp.zeros((SUBLANES, LANES), F32)
    grp = lambda cg: slice(cg * SUBLANES, (cg + 1) * SUBLANES)

    for i in range(c):
        diag = i // SUBLANES
        acc = [zero] * (diag + 1)
        acc[diag] = (sub == i % SUBLANES).astype(F32)
        for j in range(i):
            lij = jnp.broadcast_to(l_ref[i, j:j + 1, :], (SUBLANES, LANES))
            for cg in range(j // SUBLANES + 1):
                acc[cg] = acc[cg] - lij * t_ref[j, grp(cg), :]
        for cg in range(diag + 1):
            t_ref[i, grp(cg), :] = acc[cg]

    for i in reversed(range(c)):
        diag = i // SUBLANES
        acc = {cg: zero for cg in range(diag, groups)}
        acc[diag] = (sub == i % SUBLANES).astype(F32)
        for j in range(i + 1, c):
            lij = jnp.broadcast_to(l_ref[i, j:j + 1, :], (SUBLANES, LANES))
            for cg in range(j // SUBLANES, groups):
                acc[cg] = acc[cg] - lij * u_scr[j, grp(cg), :]
        for cg in range(diag, groups):
            u_scr[i, grp(cg), :] = acc[cg]

    for i in range(c):
        diag = i // SUBLANES
        for cg in range(diag + 1, groups):
            t_ref[i, grp(cg), :] = u_scr[i, grp(cg), :]
        t_ref[i, grp(diag), :] = jnp.where(sub < i % SUBLANES, t_ref[i, grp(diag), :],
                                           jnp.where(sub > i % SUBLANES, u_scr[i, grp(diag), :], 0.0))


def _dn_rec_body(t_ref, vk_ref, qg_ref, kt_ref, qk_ref, gl_ref, gate_ref, ng_ref, o_ref, state,
                 *, tl, heads):
    c = DN_CHUNK
    dk = DN_HEAD_DIM

    @pl.when(pl.program_id(1) == 0)
    def _():
        state[...] = jnp.zeros_like(state)

    chunks = range(tl // c)
    rows = lambda n: slice(n * c, (n + 1) * c)
    hcol = lambda h: slice(h * dk, (h + 1) * dk)
    mcol = lambda h: slice(h * c, (h + 1) * c)
    ri = lax.broadcasted_iota(jnp.int32, (c, c), 0)
    ci = lax.broadcasted_iota(jnp.int32, (c, c), 1)

    uw, uwb, kuw = {}, {}, {}
    for n in chunks:
        for h in range(heads):
            packed = t_ref[rows(n), mcol(h // 2)]
            vk = vk_ref[rows(n), 2 * h * dk:2 * (h + 1) * dk]
            if h % 2 == 0:
                uw[n, h] = _dot(jnp.where(ri > ci, packed, 0.0).astype(BF16), vk) + vk.astype(F32)
            else:
                uw[n, h] = _dot_tn(jnp.where(ri < ci, packed, 0.0).astype(BF16), vk) + vk.astype(F32)
            uwb[n, h] = uw[n, h].astype(BF16)
    for n in chunks:
        for h in range(heads):
            kuw[n, h] = _dot_tn(kt_ref[rows(n), hcol(h)], uwb[n, h])

    def emit(n, h, wq, v_bf16):
        o = wq[c:] + _dot(qk_ref[rows(n), mcol(h)].astype(BF16), v_bf16)
        o = o * lax.rsqrt(jnp.mean(o * o, axis=-1, keepdims=True) + RMS_EPS) * ng_ref[...]
        gt = gate_ref[rows(n), hcol(h)]
        o_ref[rows(n), hcol(h)] = (o * (gt * jax.nn.sigmoid(gt))).astype(o_ref.dtype)

    s = [state[h] for h in range(heads)]
    pending = []
    for n in chunks:
        sb = [s[h].astype(BF16) for h in range(heads)]
        for h in range(heads):
            s[h] = (s[h] * gl_ref[n * heads + h:n * heads + h + 1, :] + kuw[n, h][:, :dk]
                    - _dot(kuw[n, h][:, dk:].astype(BF16), sb[h]))
        wq = []
        for h in range(heads):
            lhs = jnp.concatenate([uwb[n, h][:, dk:], qg_ref[rows(n), hcol(h)]], axis=0)
            wq.append(_dot(lhs, sb[h]))
        for args in pending:
            emit(*args)
        pending = [(n, h, wq[h], (uw[n, h][:, :dk] - wq[h][:c]).astype(BF16)) for h in range(heads)]
    for args in pending:
        emit(*args)
    for h in range(heads):
        state[h] = s[h]


def _dn(proj, tail, bsz, seq, width, col_block, conv_w, a_log, dt_bias, norm_g, tl=512):
    heads = width // DN_HEAD_DIM
    c = DN_CHUNK
    n = bsz * seq
    tl = min(tl, seq)
    nl = seq // tl
    chunks = n // c
    cw = conv_w.astype(F32)
    pad = lambda a: jnp.pad(a.astype(F32), (0, LANES - heads)).reshape(1, LANES)
    alog = pad(a_log)
    dtb = pad(dt_bias)
    ng = norm_g.astype(F32).reshape(1, DN_HEAD_DIM)
    full = lambda a: pl.BlockSpec(a.shape, lambda b, l: (0,) * a.ndim)
    slab = lambda k: pl.BlockSpec((tl, width), lambda b, l: (b * nl + l, col_block + k))
    rows = lambda wd: pl.BlockSpec((tl, wd), lambda b, l: (b * nl + l, 0))
    gl_spec = pl.BlockSpec((tl // c * heads, LANES), lambda b, l: (b * nl + l, 0))
    wide = jax.ShapeDtypeStruct((n, width), BF16)
    mats = jax.ShapeDtypeStruct((n, heads * c), F32)
    pairs = heads // 2
    packed = jax.ShapeDtypeStruct((n, pairs * c), F32)
    vk, qg, kt, qk, low, gl = pl.pallas_call(
        functools.partial(_dn_prep_body, tl=tl, heads=heads),
        grid=(bsz, nl),
        in_specs=[slab(0), slab(1), slab(2), rows(LANES), full(cw), full(alog), full(dtb)],
        out_specs=[rows(2 * width)] + [rows(width)] * 2 + [rows(heads * c), rows(pairs * c), gl_spec],
        out_shape=[jax.ShapeDtypeStruct((n, 2 * width), BF16)] + [wide] * 2 + [mats, packed]
        + [jax.ShapeDtypeStruct((chunks * heads, LANES), F32)],
        scratch_shapes=[pltpu.VMEM((3, tl + DN_HALO, width), F32)],
        compiler_params=_params(("arbitrary", "arbitrary"), 40),
        name="deltanet_prep",
    )(proj, proj, proj, tail, cw, alog, dtb)

    m_total = chunks * pairs
    low_t = low.reshape(chunks, c, pairs, c).transpose(1, 3, 0, 2).reshape(c, c, m_total)
    t_t = pl.pallas_call(
        _tri_inv_body,
        grid=(m_total // LANES,),
        in_specs=[pl.BlockSpec((c, c, LANES), lambda i: (0, 0, i))],
        out_specs=pl.BlockSpec((c, c, LANES), lambda i: (0, 0, i)),
        out_shape=jax.ShapeDtypeStruct((c, c, m_total), F32),
        scratch_shapes=[pltpu.VMEM((c, c, LANES), F32)],
        compiler_params=_params(("parallel",), 40),
        name="deltanet_tri_inv",
    )(low_t)
    t_mat = t_t.reshape(c, c, chunks, pairs).transpose(2, 0, 3, 1).reshape(n, pairs * c)

    return pl.pallas_call(
        functools.partial(_dn_rec_body, tl=tl, heads=heads),
        grid=(bsz, nl),
        in_specs=[rows(pairs * c), rows(2 * width)] + [rows(width)] * 2
        + [rows(heads * c), gl_spec, slab(3), full(ng)],
        out_specs=rows(width),
        out_shape=wide,
        scratch_shapes=[pltpu.VMEM((heads, DN_HEAD_DIM, DN_HEAD_DIM), F32)],
        compiler_params=_params(("arbitrary", "arbitrary"), 40),
        name="deltanet_rec",
    )(t_mat, vk, qg, kt, qk, gl, proj, ng)


def _out_body(y0_ref, y1_ref, y2_ref, y3_ref, w_ref, x_ref, g_ref, b_ref, o_ref, ob_ref, *, alpha):
    width = y0_ref.shape[1]
    half = x_ref.shape[0] // 2
    accs = []
    for rows in (slice(0, half), slice(half, 2 * half)):
        mixed = jnp.concatenate([ref[rows, :] for ref in (y0_ref, y1_ref, y2_ref, y3_ref)], axis=1)
        accs.append((rows, alpha * x_ref[rows, :] + _dot(mixed, w_ref[...])))
    for rows, acc in accs:
        y = _layer_norm(acc, g_ref[...], b_ref[...])
        o_ref[rows, :] = y
        ob_ref[rows, :] = y.astype(BF16)


def _out_proj(ys, w_out, x, g, b, alpha, tm=512):
    n, d = x.shape
    width = ys[0].shape[1]
    tm = min(tm, n)
    row = lambda wd: pl.BlockSpec((tm, wd), lambda i: (i, 0))
    full = lambda a: pl.BlockSpec(a.shape, lambda i: (0,) * a.ndim)
    g = g.astype(F32).reshape(1, d)
    b = b.astype(F32).reshape(1, d)
    return pl.pallas_call(
        functools.partial(_out_body, alpha=alpha),
        grid=(n // tm,),
        in_specs=[row(width)] * 4 + [full(w_out), row(d), full(g), full(b)],
        out_specs=[row(d), row(d)],
        out_shape=[jax.ShapeDtypeStruct((n, d), F32), jax.ShapeDtypeStruct((n, d), BF16)],
        compiler_params=_params(("parallel",), 56),
        name="out_proj_ln",
    )(*ys, w_out, x, g, b)


def _ffn_body(xb_ref, wu_ref, wd_ref, x_ref, g_ref, b_ref, o_ref, ob_ref, acc, *, alpha):
    f = pl.program_id(1)

    @pl.when(f == 0)
    def _():
        acc[...] = alpha * x_ref[...]

    hidden = jnp.maximum(_dot(xb_ref[...], wu_ref[...]), 0.0)
    acc[...] += _dot((hidden * hidden).astype(BF16), wd_ref[...])

    @pl.when(f == pl.num_programs(1) - 1)
    def _():
        y = _layer_norm(acc[...], g_ref[...], b_ref[...])
        o_ref[...] = y
        ob_ref[...] = y.astype(BF16)


def _ffn(xb, x, w_up, w_down, g, b, alpha, tm=512, tf=1024):
    n, d = x.shape
    ff = w_up.shape[1]
    tm = min(tm, n)
    g = g.astype(F32).reshape(1, d)
    b = b.astype(F32).reshape(1, d)
    row = pl.BlockSpec((tm, d), lambda i, f: (i, 0))
    vec = pl.BlockSpec((1, d), lambda i, f: (0, 0))
    return pl.pallas_call(
        functools.partial(_ffn_body, alpha=alpha),
        grid=(n // tm, ff // tf),
        in_specs=[row, pl.BlockSpec((d, tf), lambda i, f: (0, f)), pl.BlockSpec((tf, d), lambda i, f: (f, 0)),
                  row, vec, vec],
        out_specs=[row, row],
        out_shape=[jax.ShapeDtypeStruct((n, d), F32), jax.ShapeDtypeStruct((n, d), BF16)],
        scratch_shapes=[pltpu.VMEM((tm, d), F32)],
        compiler_params=_params(("parallel", "arbitrary"), 56),
        name="ffn_ln",
    )(xb, w_up, w_down, x, g, b)


def kernel(x, w_in, s5_lambda_re, s5_lambda_im, s5_log_step, s5_b_re, s5_b_im, s5_c_re, s5_c_im, s5_d, s5_glu_w, s5_glu_b, sgu_norm_g, sgu_norm_b, sgu_w, sgu_b, pool_w, pool_scale, dn_conv_w, dn_a_log, dn_dt_bias, dn_norm_g, w_out, ln1_g, ln1_b, w_up, w_down, ln2_g, ln2_b):
    bsz, seq, d = x.shape
    depth = w_in.shape[0]
    n = bsz * seq
    width = s5_glu_w.shape[1]
    heads = dn_a_log.shape[1]
    main_cols = 8 * width
    alpha = (2 * depth) ** 0.25
    s5_sub = min(256, seq)

    xf = x.reshape(n, d).astype(F32)
    xb = xf
    w_in_t = jnp.swapaxes(w_in, 1, 2)
    for i in range(depth):
        proj, tail = _proj(xb, _cast_bf16(w_in_t, i, 3 * LANES, pad_rows_to=main_cols + LANES), main_cols)
        prep = _s5_prepare(s5_lambda_re[i], s5_lambda_im[i], s5_log_step[i], s5_b_re[i], s5_b_im[i],
                           s5_c_re[i], s5_c_im[i], s5_d[i], s5_sub // SUBLANES)
        y_s5 = _s5(proj, bsz, seq, prep, s5_glu_w[i], s5_glu_b[i], s5_sub)
        y_sgu = _sgu(proj, n, width, sgu_norm_g[i], sgu_norm_b[i], sgu_w[i], sgu_b[i])
        y_pool = _pool(proj, bsz, seq, width, pool_w[i], pool_scale[i], col_block=3)
        y_dn = _dn(proj, tail, bsz, seq, width, 4, dn_conv_w[i], dn_a_log[i], dn_dt_bias[i], dn_norm_g[i])
        xf, xb = _out_proj((y_s5, y_sgu, y_pool, y_dn), _cast_bf16(w_out, i, 1024), xf, ln1_g[i], ln1_b[i], alpha)
        xf, xb = _ffn(xb, xf, _cast_bf16(w_up, i, 256), _cast_bf16(w_down, i, 1024), ln2_g[i], ln2_b[i], alpha)
    return xf.reshape(bsz, seq, d).astype(x.dtype)
```

```python
import functools
import math

import jax
import jax.numpy as jnp
from jax import lax
from jax.experimental import pallas as pl
from jax.experimental.pallas import tpu as pltpu

F32 = jnp.float32
BF16 = jnp.bfloat16
HIGHEST = lax.Precision.HIGHEST

LANES = 128
SUBLANES = 8

S5_CH_PER_GROUP = 16
S5_STATE = 64
S5_GROUPS_PER_SLAB = LANES // S5_CH_PER_GROUP
S5_SLAB_STATE = S5_GROUPS_PER_SLAB * S5_STATE
SGU_CHUNK = 128
SGU_HEAD_DIM = 64
POOL_WINDOWS = (2, 4, 8, 16)
POOL_HALO = 16
DN_HEAD_DIM = 128
DN_CONV = 4
DN_CHUNK = 64
DN_HALO = 8
LN_EPS = 1e-5
RMS_EPS = 1e-6
L2_EPS = 1e-6


def _params(semantics, vmem_mib):
    return pltpu.CompilerParams(dimension_semantics=semantics, vmem_limit_bytes=vmem_mib * 1024 * 1024)


def _layer_norm(y, g, b):
    mu = jnp.mean(y, axis=-1, keepdims=True)
    yc = y - mu
    var = jnp.mean(yc * yc, axis=-1, keepdims=True)
    return yc * lax.rsqrt(var + LN_EPS) * g + b


def _dot(a, b):
    return jnp.dot(a, b, preferred_element_type=F32)


def _dot_nt(a, b, precision=None):
    return lax.dot_general(a, b, (((1,), (1,)), ((), ())), precision=precision, preferred_element_type=F32)


def _dot_tn(a, b):
    return lax.dot_general(a, b, (((0,), (0,)), ((), ())), preferred_element_type=F32)


def _split3(x):
    hi = x.astype(BF16)
    rest = x - hi.astype(F32)
    mid = rest.astype(BF16)
    lo = (rest - mid.astype(F32)).astype(BF16)
    return hi, mid, lo


def _proj_body(x_ref, w_ref, wt_ref, *refs, n_cast):
    cast_in, (o_ref, ot_ref) = refs[:n_cast], refs[n_cast:n_cast + 2]
    cast_out, scratch = refs[n_cast + 2:2 * n_cast + 2], refs[2 * n_cast + 2:]
    if scratch:
        xb_ref, = scratch

        @pl.when(pl.program_id(1) == 0)
        def _():
            xb_ref[...] = x_ref[...].astype(BF16)
    else:
        xb_ref = x_ref
    o_ref[...] = _dot_nt(xb_ref[...], w_ref[...])
    for src, dst in zip(cast_in, cast_out):
        dst[...] = src[...].astype(BF16)

    @pl.when(pl.program_id(1) == 0)
    def _():
        ot_ref[...] = _dot_nt(xb_ref[...], wt_ref[...])


def _proj(x, w_t, cols, stacked, layer, tm=1024, tn=1024):
    n, d = x.shape
    tm = min(tm, n)
    ni, nj = n // tm, cols // tn
    scratch = [] if x.dtype == BF16 else [pltpu.VMEM((tm, d), BF16)]
    cast_rows = [w.shape[1] // (ni * nj) for w in stacked]
    cast_in = [pl.BlockSpec((None, r, w.shape[2]), lambda i, j: (layer, i * nj + j, 0))
               for w, r in zip(stacked, cast_rows)]
    cast_out = [pl.BlockSpec((r, w.shape[2]), lambda i, j: (i * nj + j, 0)) for w, r in zip(stacked, cast_rows)]
    return pl.pallas_call(
        functools.partial(_proj_body, n_cast=len(stacked)),
        grid=(ni, nj),
        in_specs=[pl.BlockSpec((tm, d), lambda i, j: (i, 0)),
                  pl.BlockSpec((tn, d), lambda i, j: (j, 0)),
                  pl.BlockSpec((LANES, d), lambda i, j: (cols // LANES, 0))] + cast_in,
        out_specs=[pl.BlockSpec((tm, tn), lambda i, j: (i, j)),
                   pl.BlockSpec((tm, LANES), lambda i, j: (i, 0))] + cast_out,
        out_shape=[jax.ShapeDtypeStruct((n, cols), F32), jax.ShapeDtypeStruct((n, LANES), F32)]
        + [jax.ShapeDtypeStruct(w.shape[1:], BF16) for w in stacked],
        scratch_shapes=scratch,
        compiler_params=_params(("parallel", "arbitrary"), 56),
        name="proj",
    )(x, w_t, w_t, *stacked)


def _cast_body(w_ref, o_ref, *, valid_rows):
    w = w_ref[...]
    if valid_rows % w.shape[0]:
        row = pl.program_id(0) * w.shape[0] + lax.broadcasted_iota(jnp.int32, w.shape, 0)
        w = jnp.where(row < valid_rows, w, 0.0)
    o_ref[...] = w.astype(o_ref.dtype)


def _cast_bf16(w, layer, block_rows, pad_rows_to=None):
    _, r, c = w.shape
    out_r = pad_rows_to or r
    return pl.pallas_call(
        functools.partial(_cast_body, valid_rows=r),
        grid=(out_r // block_rows,),
        in_specs=[pl.BlockSpec((None, block_rows, c), lambda i: (layer, i, 0))],
        out_specs=pl.BlockSpec((block_rows, c), lambda i: (i, 0)),
        out_shape=jax.ShapeDtypeStruct((out_r, c), BF16),
        compiler_params=_params(("parallel",), 40),
        name="cast_bf16",
    )(w)


def _s5_body(u_ref, perm_ref, unperm_ref, wb_ref, wc_ref, are_ref, aim_ref, pre_ref, pim_ref, d_ref, gw_ref, gb_ref,
             o_ref, sre, sim, hre, him, y_scr, *, sub, steps):
    n_slabs = u_ref.shape[1] // LANES
    n_sub = u_ref.shape[0] // sub
    sw = S5_SLAB_STATE
    tiles = sw // LANES

    @pl.when(pl.program_id(1) == 0)
    def _():
        hre[...] = jnp.zeros_like(hre)
        him[...] = jnp.zeros_like(him)

    perm = perm_ref[...]
    us = []
    for a in range(n_sub):
        u1, u2, u3 = _split3(u_ref[a * sub:(a + 1) * sub, :])
        u = _dot(perm, u1) + _dot(perm, u2) + _dot(perm, u3)
        us.append(u)
        ub = u.astype(BF16)
        for j in range(n_slabs):
            bu = _dot(ub[:, j * LANES:(j + 1) * LANES], wb_ref[j])
            for k in range(tiles):
                sre[a, j * tiles + k] = bu[:, k * LANES:(k + 1) * LANES]
                sim[a, j * tiles + k] = bu[:, sw + k * LANES:sw + (k + 1) * LANES]

    for a in range(n_sub):
        out_rows = slice(a * sub, (a + 1) * sub)
        for j in range(n_slabs):
            ts = [j * tiles + k for k in range(tiles)]
            cols = [slice(t * LANES, (t + 1) * LANES) for t in ts]

            ar = [jnp.broadcast_to(are_ref[:, c], (SUBLANES, LANES)) for c in cols]
            ai = [jnp.broadcast_to(aim_ref[:, c], (SUBLANES, LANES)) for c in cols]
            hr = [jnp.zeros((SUBLANES, LANES), F32)] * tiles
            hi = [jnp.zeros((SUBLANES, LANES), F32)] * tiles
            for r in range(steps):
                rows = slice(r * SUBLANES, (r + 1) * SUBLANES)
                for k, t in enumerate(ts):
                    hr[k], hi[k] = (ar[k] * hr[k] - ai[k] * hi[k] + sre[a, t, rows, :],
                                    ar[k] * hi[k] + ai[k] * hr[k] + sim[a, t, rows, :])
                    sre[a, t, rows, :] = hr[k]
                    sim[a, t, rows, :] = hi[k]

            cr8, ci8 = [], []
            for k in range(tiles):
                apr = pre_ref[steps - 1:steps, cols[k]]
                api = pim_ref[steps - 1:steps, cols[k]]
                cr = hre[:, cols[k]]
                ci = him[:, cols[k]]
                crs, cis = [], []
                for s in range(SUBLANES):
                    crs.append(cr)
                    cis.append(ci)
                    cr, ci = (apr * cr - api * ci + hr[k][s:s + 1, :], apr * ci + api * cr + hi[k][s:s + 1, :])
                hre[:, cols[k]] = cr
                him[:, cols[k]] = ci
                cr8.append(jnp.concatenate(crs, axis=0))
                ci8.append(jnp.concatenate(cis, axis=0))
            for r in range(steps):
                rows = slice(r * SUBLANES, (r + 1) * SUBLANES)
                for k, t in enumerate(ts):
                    pr = jnp.broadcast_to(pre_ref[r:r + 1, cols[k]], (SUBLANES, LANES))
                    pi = jnp.broadcast_to(pim_ref[r:r + 1, cols[k]], (SUBLANES, LANES))
                    sre[a, t, rows, :] = sre[a, t, rows, :] + (pr * cr8[k] - pi * ci8[k])
                    sim[a, t, rows, :] = sim[a, t, rows, :] + (pr * ci8[k] + pi * cr8[k])

            h_re = jnp.concatenate([sre[a, t] for t in ts], axis=1).astype(BF16)
            h_im = jnp.concatenate([sim[a, t] for t in ts], axis=1).astype(BF16)
            y = _dot(h_re, wc_ref[j, :sw, :]) + _dot(h_im, wc_ref[j, sw:, :])
            lc = slice(j * LANES, (j + 1) * LANES)
            y = y + d_ref[:, lc] * us[a][:, lc]
            y_scr[out_rows, lc] = jax.nn.gelu(y)

        y = y_scr[out_rows, :]
        gate = _dot(y.astype(BF16), gw_ref[...]) + gb_ref[...]
        out = (y * jax.nn.sigmoid(gate)).astype(BF16)
        o_ref[out_rows, :] = _dot(unperm_ref[...], out).astype(o_ref.dtype)


def _s5_prepare(lam_re, lam_im, log_step, b_re, b_im, c_re, c_im, d, steps):
    g, p = lam_re.shape
    h = S5_CH_PER_GROUP
    q = S5_GROUPS_PER_SLAB
    n_slabs = g // q
    step = jnp.exp(log_step.astype(F32))[:, None]
    lr, li = lam_re.astype(F32), lam_im.astype(F32)
    mag = jnp.exp(lr * step)
    a_re, a_im = mag * jnp.cos(li * step), mag * jnp.sin(li * step)
    den = lr * lr + li * li
    f_re = ((a_re - 1.0) * lr + a_im * li) / den
    f_im = (a_im * lr - (a_re - 1.0) * li) / den
    bb_re = f_re[:, :, None] * b_re - f_im[:, :, None] * b_im
    bb_im = f_re[:, :, None] * b_im + f_im[:, :, None] * b_re
    eye = jnp.eye(q, dtype=F32)

    def in_block(m):
        m = m.reshape(n_slabs, q, p, h)
        return jnp.einsum('jqph,qr->jqhrp', m, eye).reshape(n_slabs, q * h, q * p)

    def out_block(m):
        m = m.reshape(n_slabs, q, h, p)
        return jnp.einsum('jqhp,qr->jqprh', m, eye).reshape(n_slabs, q * p, q * h)

    wb = jnp.concatenate([in_block(bb_re), in_block(bb_im)], axis=2).astype(BF16)
    wc = jnp.concatenate([out_block(c_re.astype(F32)), out_block(-c_im.astype(F32))], axis=1).astype(BF16)
    k = jnp.arange(1, steps + 1, dtype=F32)[:, None, None]
    pmag = jnp.exp(lr * step * k)
    ang = li * step * k
    pw_re = (pmag * jnp.cos(ang)).reshape(steps, g * p)
    pw_im = (pmag * jnp.sin(ang)).reshape(steps, g * p)
    return (wb, wc, a_re.reshape(1, g * p), a_im.reshape(1, g * p), pw_re, pw_im,
            d.astype(F32).reshape(1, g * h))


def _s5(proj, bsz, seq, prep, glu_w, glu_b, sub, n_sub=2):
    wb, wc, a_re, a_im, pw_re, pw_im, d = prep
    tl = min(sub * n_sub, seq)
    steps = sub // SUBLANES
    width = d.shape[1]
    states = pw_re.shape[1]
    nl = seq // tl
    full = lambda a: pl.BlockSpec(a.shape, lambda b, l: (0,) * a.ndim)
    gw = glu_w.astype(BF16)
    gb = glu_b.astype(F32).reshape(1, width)
    dest = jnp.arange(sub)
    src = (dest % SUBLANES) * steps + dest // SUBLANES
    perm = (src[:, None] == jnp.arange(sub)[None, :]).astype(BF16)
    unperm = perm.T
    state_scratch = pltpu.VMEM((tl // sub, states // LANES, sub, LANES), F32)
    return pl.pallas_call(
        functools.partial(_s5_body, sub=sub, steps=steps),
        grid=(bsz, nl),
        in_specs=[pl.BlockSpec((tl, width), lambda b, l: (b * nl + l, 0)), full(perm), full(unperm),
                  full(wb), full(wc), full(a_re), full(a_im), full(pw_re), full(pw_im), full(d), full(gw), full(gb)],
        out_specs=pl.BlockSpec((tl, width), lambda b, l: (b * nl + l, 0)),
        out_shape=jax.ShapeDtypeStruct((bsz * seq, width), BF16),
        scratch_shapes=[state_scratch, state_scratch,
                        pltpu.VMEM((1, states), F32), pltpu.VMEM((1, states), F32),
                        pltpu.VMEM((tl, width), F32)],
        compiler_params=_params(("arbitrary", "arbitrary"), 40),
        name="s5",
    )(proj, perm, unperm, wb, wc, a_re, a_im, pw_re, pw_im, d, gw, gb)


def _sgu_body(zu_ref, zv_ref, g_ref, b_ref, w_ref, bias_ref, o_ref, *, tl):
    width = zu_ref.shape[1]
    u = jax.nn.gelu(zu_ref[...])
    v = _layer_norm(jax.nn.gelu(zv_ref[...]), g_ref[...], b_ref[...])
    vb = v.astype(BF16)
    row = lax.broadcasted_iota(jnp.int32, (2 * SGU_CHUNK, SGU_CHUNK), 0)
    col = lax.broadcasted_iota(jnp.int32, (2 * SGU_CHUNK, SGU_CHUNK), 1)
    causal = col <= (row % SGU_CHUNK)
    lane = lax.broadcasted_iota(jnp.int32, (SGU_CHUNK, LANES), 1)
    first = lane < SGU_HEAD_DIM
    for j in range(width // LANES):
        w = jnp.where(causal, w_ref[j], 0.0).astype(BF16)
        bias = bias_ref[j]
        for c in range(tl // SGU_CHUNK):
            rs = slice(c * SGU_CHUNK, (c + 1) * SGU_CHUNK)
            cs = slice(j * LANES, (j + 1) * LANES)
            r = _dot(w, vb[rs, cs])
            mixed = jnp.where(first, r[:SGU_CHUNK], r[SGU_CHUNK:]) + bias
            o_ref[rs, cs] = (u[rs, cs] * mixed).astype(o_ref.dtype)


def _sgu(proj, n, width, norm_g, norm_b, w_s, b_s, tl=512):
    heads = w_s.shape[0]
    tl = min(tl, n)
    w_pairs = w_s.astype(F32).reshape(heads // 2, 2 * SGU_CHUNK, SGU_CHUNK)
    bias = jnp.repeat(b_s.astype(F32).T, SGU_HEAD_DIM, axis=1)
    bias = bias.reshape(SGU_CHUNK, heads // 2, LANES).transpose(1, 0, 2)
    g = norm_g.astype(F32).reshape(1, width)
    b = norm_b.astype(F32).reshape(1, width)
    full = lambda a: pl.BlockSpec(a.shape, lambda i: (0,) * a.ndim)
    return pl.pallas_call(
        functools.partial(_sgu_body, tl=tl),
        grid=(n // tl,),
        in_specs=[pl.BlockSpec((tl, width), lambda i: (i, 1)),
                  pl.BlockSpec((tl, width), lambda i: (i, 2)),
                  full(g), full(b), full(w_pairs), full(bias)],
        out_specs=pl.BlockSpec((tl, width), lambda i: (i, 0)),
        out_shape=jax.ShapeDtypeStruct((n, width), BF16),
        compiler_params=_params(("parallel",), 40),
        name="sgu",
    )(proj, proj, g, b, w_pairs, bias)


def _pool_body(p_ref, w_ref, sc_ref, o_ref, ext, *, tl):
    l = pl.program_id(1)

    @pl.when(l == 0)
    def _():
        ext[0:POOL_HALO, :] = jnp.zeros((POOL_HALO, ext.shape[1]), F32)

    @pl.when(l > 0)
    def _():
        ext[0:POOL_HALO, :] = ext[tl:tl + POOL_HALO, :]

    ext[POOL_HALO:POOL_HALO + tl, :] = p_ref[...]
    pos = l * tl + lax.broadcasted_iota(jnp.int32, (tl, LANES), 0)
    for gi, win in enumerate(POOL_WINDOWS):
        cs = slice(gi * LANES, (gi + 1) * LANES)
        x = ext[POOL_HALO:POOL_HALO + tl, cs]
        acc = ext[:, cs]
        span = 1
        while span < win:
            acc = acc + pltpu.roll(acc, span, 0)
            span *= 2
        acc = acc[POOL_HALO:, :]
        count = jnp.minimum(pos + 1, win).astype(F32)
        pooled = acc / count - x
        y = _dot(pooled.astype(BF16), w_ref[gi]) * sc_ref[:, cs]
        o_ref[:, cs] = y.astype(o_ref.dtype)


def _pool(proj, bsz, seq, width, w_pool, scale, col_block, tl=512):
    tl = min(tl, seq)
    nl = seq // tl
    w = w_pool.astype(BF16)
    sc = scale.astype(F32).reshape(1, width)
    return pl.pallas_call(
        functools.partial(_pool_body, tl=tl),
        grid=(bsz, nl),
        in_specs=[pl.BlockSpec((tl, width), lambda b, l: (b * nl + l, col_block)),
                  pl.BlockSpec(w.shape, lambda b, l: (0, 0, 0)),
                  pl.BlockSpec(sc.shape, lambda b, l: (0, 0))],
        out_specs=pl.BlockSpec((tl, width), lambda b, l: (b * nl + l, 0)),
        out_shape=jax.ShapeDtypeStruct((bsz * seq, width), BF16),
        scratch_shapes=[pltpu.VMEM((tl + POOL_HALO, width), F32)],
        compiler_params=_params(("arbitrary", "arbitrary"), 40),
        name="pool",
    )(proj, w, sc)


def _dn_prep_body(q_ref, k_ref, v_ref, tail_ref, cw_ref, alog_ref, dtb_ref,
                  vk_ref, qg_ref, kt_ref, qk_ref, low_ref, gl_ref, ext, *, tl, heads):
    l = pl.program_id(1)
    width = heads * DN_HEAD_DIM
    c = DN_CHUNK

    @pl.when(l == 0)
    def _():
        ext[:, 0:DN_HALO, :] = jnp.zeros((3, DN_HALO, width), F32)

    @pl.when(l > 0)
    def _():
        ext[:, 0:DN_HALO, :] = ext[:, tl:tl + DN_HALO, :]

    mixed = []
    for i, ref in enumerate((q_ref, k_ref, v_ref)):
        ext[i, DN_HALO:DN_HALO + tl, :] = ref[...]
        e = ext[i]
        acc = e * cw_ref[0:1, i * width:(i + 1) * width]
        for j in range(1, DN_CONV):
            acc = pltpu.roll(acc, 1, 0) + e * cw_ref[j:j + 1, i * width:(i + 1) * width]
        acc = acc[DN_HALO:, :]
        mixed.append(acc * jax.nn.sigmoid(acc))
    qc, kc, vc = mixed

    tail = tail_ref[...]
    sp_in = tail + dtb_ref[...]
    softplus = jnp.maximum(sp_in, 0.0) + jnp.log1p(jnp.exp(-jnp.abs(sp_in)))
    g_all = -jnp.exp(alog_ref[...]) * softplus
    beta_all = jax.nn.sigmoid(tail)

    ri = lax.broadcasted_iota(jnp.int32, (c, c), 0)
    ci = lax.broadcasted_iota(jnp.int32, (c, c), 1)
    causal = ri >= ci
    strict = ri > ci
    tri = causal.astype(BF16)
    sel = (lax.broadcasted_iota(jnp.int32, (heads * c, LANES), 0) // c
           == lax.broadcasted_iota(jnp.int32, (heads * c, LANES), 1)).astype(BF16)
    chunks = range(tl // c)

    gc = []
    for n in chunks:
        g1, g2, g3 = _split3(g_all[n * c:(n + 1) * c, :])
        gc.append(_dot(tri, g1) + _dot(tri, g2) + _dot(tri, g3))
    grow = []
    for n in chunks:
        g1, g2, g3 = _split3(gc[n])
        grow.append(_dot_nt(sel, g1) + _dot_nt(sel, g2) + _dot_nt(sel, g3))
    for n in chunks:
        rs = slice(n * c, (n + 1) * c)
        qn, kn, kbeta, kq, kkt = {}, {}, {}, {}, {}
        for h in range(heads):
            hs = slice(h * DN_HEAD_DIM, (h + 1) * DN_HEAD_DIM)
            q = qc[rs, hs]
            k = kc[rs, hs]
            q = q * lax.rsqrt(jnp.sum(q * q, axis=-1, keepdims=True) + L2_EPS) * (DN_HEAD_DIM ** -0.5)
            k = k * lax.rsqrt(jnp.sum(k * k, axis=-1, keepdims=True) + L2_EPS)
            kb = k * beta_all[rs, heads + h:heads + h + 1]
            qn[n, h], kn[n, h], kbeta[n, h] = q, k, kb
            if h % 2 == 0:
                kq[n, h] = _dot_nt(jnp.concatenate([kb.astype(BF16), q.astype(BF16)], axis=0), k.astype(BF16))
            else:
                kq[n, h] = _dot_nt(q.astype(BF16), k.astype(BF16))
                kkt[n, h] = _dot_nt(k.astype(BF16), kb.astype(BF16))
        for h in range(heads):
            hs = slice(h * DN_HEAD_DIM, (h + 1) * DN_HEAD_DIM)
            ms = slice(h * c, (h + 1) * c)
            gcol = gc[n][:, h:h + 1]
            glast = gc[n][c - 1:c, h:h + 1]
            decay = jnp.exp(-jnp.abs(gcol - grow[n][ms, :]))
            egc = jnp.exp(gcol)
            if h % 2 == 0:
                packed = jnp.where(strict, kq[n, h][:c] * decay, 0.0)
                qk_ref[rs, ms] = jnp.where(causal, kq[n, h][c:] * decay, 0.0)
            else:
                low_ref[rs, (h // 2) * c:(h // 2 + 1) * c] = jnp.where(ri < ci, kkt[n, h] * decay, packed)
                qk_ref[rs, ms] = jnp.where(causal, kq[n, h] * decay, 0.0)
            beta = beta_all[rs, heads + h:heads + h + 1]
            vk_ref[rs, 2 * h * DN_HEAD_DIM:(2 * h + 1) * DN_HEAD_DIM] = (vc[rs, hs] * beta).astype(BF16)
            vk_ref[rs, (2 * h + 1) * DN_HEAD_DIM:(2 * h + 2) * DN_HEAD_DIM] = (kbeta[n, h] * egc).astype(BF16)
            qg_ref[rs, hs] = (qn[n, h] * egc).astype(BF16)
            kt_ref[rs, hs] = (kn[n, h] * jnp.exp(glast - gcol)).astype(BF16)
            gl_ref[n * heads + h:n * heads + h + 1, :] = jnp.broadcast_to(jnp.exp(glast), (1, LANES))


def _tri_inv_body(l_ref, t_ref, u_scr):
    c = l_ref.shape[0]
    groups = c // SUBLANES
    sub = lax.broadcasted_iota(jnp.int32, (SUBLANES, LANES), 0)
    zero = jnp.zeros((SUBLANES, LANES), F32)
    grp = lambda cg: slice(cg * SUBLANES, (cg + 1) * SUBLANES)

    for i in range(c):
        diag = i // SUBLANES
        acc = [zero] * (diag + 1)
        acc[diag] = (sub == i % SUBLANES).astype(F32)
        for j in range(i):
            lij = jnp.broadcast_to(l_ref[i, j:j + 1, :], (SUBLANES, LANES))
            for cg in range(j // SUBLANES + 1):
                acc[cg] = acc[cg] - lij * t_ref[j, grp(cg), :]
        for cg in range(diag + 1):
            t_ref[i, grp(cg), :] = acc[cg]

    for i in reversed(range(c)):
        diag = i // SUBLANES
        acc = {cg: zero for cg in range(diag, groups)}
        acc[diag] = (sub == i % SUBLANES).astype(F32)
        for j in range(i + 1, c):
            lij = jnp.broadcast_to(l_ref[i, j:j + 1, :], (SUBLANES, LANES))
            for cg in range(j // SUBLANES, groups):
                acc[cg] = acc[cg] - lij * u_scr[j, grp(cg), :]
        for cg in range(diag, groups):
            u_scr[i, grp(cg), :] = acc[cg]

    for i in range(c):
        diag = i // SUBLANES
        for cg in range(diag + 1, groups):
            t_ref[i, grp(cg), :] = u_scr[i, grp(cg), :]
        t_ref[i, grp(diag), :] = jnp.where(sub < i % SUBLANES, t_ref[i, grp(diag), :],
                                           jnp.where(sub > i % SUBLANES, u_scr[i, grp(diag), :], 0.0))


def _dn_rec_body(t_ref, vk_ref, qg_ref, kt_ref, qk_ref, gl_ref, gate_ref, ng_ref, o_ref, state,
                 *, tl, heads):
    c = DN_CHUNK
    dk = DN_HEAD_DIM

    @pl.when(pl.program_id(1) == 0)
    def _():
        state[...] = jnp.zeros_like(state)

    chunks = range(tl // c)
    rows = lambda n: slice(n * c, (n + 1) * c)
    hcol = lambda h: slice(h * dk, (h + 1) * dk)
    mcol = lambda h: slice(h * c, (h + 1) * c)
    ri = lax.broadcasted_iota(jnp.int32, (c, c), 0)
    ci = lax.broadcasted_iota(jnp.int32, (c, c), 1)

    uw, uwb, kuw = {}, {}, {}
    for n in chunks:
        for h in range(heads):
            packed = t_ref[rows(n), mcol(h // 2)]
            vk = vk_ref[rows(n), 2 * h * dk:2 * (h + 1) * dk]
            if h % 2 == 0:
                uw[n, h] = _dot(jnp.where(ri > ci, packed, 0.0).astype(BF16), vk) + vk.astype(F32)
            else:
                uw[n, h] = _dot_tn(jnp.where(ri < ci, packed, 0.0).astype(BF16), vk) + vk.astype(F32)
            uwb[n, h] = uw[n, h].astype(BF16)
    for n in chunks:
        for h in range(heads):
            kuw[n, h] = _dot_tn(kt_ref[rows(n), hcol(h)], uwb[n, h])

    def emit(n, h, wq, v_bf16):
        o = wq[c:] + _dot(qk_ref[rows(n), mcol(h)].astype(BF16), v_bf16)
        o = o * lax.rsqrt(jnp.mean(o * o, axis=-1, keepdims=True) + RMS_EPS) * ng_ref[...]
        gt = gate_ref[rows(n), hcol(h)]
        o_ref[rows(n), hcol(h)] = (o * (gt * jax.nn.sigmoid(gt))).astype(o_ref.dtype)

    s = [state[h] for h in range(heads)]
    pending = []
    for n in chunks:
        sb = [s[h].astype(BF16) for h in range(heads)]
        for h in range(heads):
            s[h] = (s[h] * gl_ref[n * heads + h:n * heads + h + 1, :] + kuw[n, h][:, :dk]
                    - _dot(kuw[n, h][:, dk:].astype(BF16), sb[h]))
        wq = []
        for h in range(heads):
            lhs = jnp.concatenate([uwb[n, h][:, dk:], qg_ref[rows(n), hcol(h)]], axis=0)
            wq.append(_dot(lhs, sb[h]))
        for args in pending:
            emit(*args)
        pending = [(n, h, wq[h], (uw[n, h][:, :dk] - wq[h][:c]).astype(BF16)) for h in range(heads)]
    for args in pending:
        emit(*args)
    for h in range(heads):
        state[h] = s[h]


def _dn(proj, tail, bsz, seq, width, col_block, conv_w, a_log, dt_bias, norm_g, tl=512):
    heads = width // DN_HEAD_DIM
    c = DN_CHUNK
    n = bsz * seq
    tl = min(tl, seq)
    nl = seq // tl
    chunks = n // c
    cw = conv_w.astype(F32)
    pad = lambda a: jnp.pad(a.astype(F32), (0, LANES - heads)).reshape(1, LANES)
    alog = pad(a_log)
    dtb = pad(dt_bias)
    ng = norm_g.astype(F32).reshape(1, DN_HEAD_DIM)
    full = lambda a: pl.BlockSpec(a.shape, lambda b, l: (0,) * a.ndim)
    slab = lambda k: pl.BlockSpec((tl, width), lambda b, l: (b * nl + l, col_block + k))
    rows = lambda wd: pl.BlockSpec((tl, wd), lambda b, l: (b * nl + l, 0))
    gl_spec = pl.BlockSpec((tl // c * heads, LANES), lambda b, l: (b * nl + l, 0))
    wide = jax.ShapeDtypeStruct((n, width), BF16)
    mats = jax.ShapeDtypeStruct((n, heads * c), F32)
    pairs = heads // 2
    packed = jax.ShapeDtypeStruct((n, pairs * c), F32)
    vk, qg, kt, qk, low, gl = pl.pallas_call(
        functools.partial(_dn_prep_body, tl=tl, heads=heads),
        grid=(bsz, nl),
        in_specs=[slab(0), slab(1), slab(2), rows(LANES), full(cw), full(alog), full(dtb)],
        out_specs=[rows(2 * width)] + [rows(width)] * 2 + [rows(heads * c), rows(pairs * c), gl_spec],
        out_shape=[jax.ShapeDtypeStruct((n, 2 * width), BF16)] + [wide] * 2 + [mats, packed]
        + [jax.ShapeDtypeStruct((chunks * heads, LANES), F32)],
        scratch_shapes=[pltpu.VMEM((3, tl + DN_HALO, width), F32)],
        compiler_params=_params(("arbitrary", "arbitrary"), 40),
        name="deltanet_prep",
    )(proj, proj, proj, tail, cw, alog, dtb)

    m_total = chunks * pairs
    low_t = low.reshape(chunks, c, pairs, c).transpose(1, 3, 0, 2).reshape(c, c, m_total)
    t_t = pl.pallas_call(
        _tri_inv_body,
        grid=(m_total // LANES,),
        in_specs=[pl.BlockSpec((c, c, LANES), lambda i: (0, 0, i))],
        out_specs=pl.BlockSpec((c, c, LANES), lambda i: (0, 0, i)),
        out_shape=jax.ShapeDtypeStruct((c, c, m_total), F32),
        scratch_shapes=[pltpu.VMEM((c, c, LANES), F32)],
        compiler_params=_params(("parallel",), 40),
        name="deltanet_tri_inv",
    )(low_t)
    t_mat = t_t.reshape(c, c, chunks, pairs).transpose(2, 0, 3, 1).reshape(n, pairs * c)

    return pl.pallas_call(
        functools.partial(_dn_rec_body, tl=tl, heads=heads),
        grid=(bsz, nl),
        in_specs=[rows(pairs * c), rows(2 * width)] + [rows(width)] * 2
        + [rows(heads * c), gl_spec, slab(3), full(ng)],
        out_specs=rows(width),
        out_shape=wide,
        scratch_shapes=[pltpu.VMEM((heads, DN_HEAD_DIM, DN_HEAD_DIM), F32)],
        compiler_params=_params(("arbitrary", "arbitrary"), 40),
        name="deltanet_rec",
    )(t_mat, vk, qg, kt, qk, gl, proj, ng)


def _out_body(y0_ref, y1_ref, y2_ref, y3_ref, w_ref, x_ref, g_ref, b_ref, o_ref, ob_ref, *, alpha):
    width = y0_ref.shape[1]
    half = x_ref.shape[0] // 2
    accs = []
    for rows in (slice(0, half), slice(half, 2 * half)):
        mixed = jnp.concatenate([ref[rows, :] for ref in (y0_ref, y1_ref, y2_ref, y3_ref)], axis=1)
        accs.append((rows, alpha * x_ref[rows, :] + _dot(mixed, w_ref[...])))
    for rows, acc in accs:
        y = _layer_norm(acc, g_ref[...], b_ref[...])
        o_ref[rows, :] = y
        ob_ref[rows, :] = y.astype(BF16)


def _out_proj(ys, w_out, x, g, b, alpha, tm=512):
    n, d = x.shape
    width = ys[0].shape[1]
    tm = min(tm, n)
    row = lambda wd: pl.BlockSpec((tm, wd), lambda i: (i, 0))
    full = lambda a: pl.BlockSpec(a.shape, lambda i: (0,) * a.ndim)
    g = g.astype(F32).reshape(1, d)
    b = b.astype(F32).reshape(1, d)
    return pl.pallas_call(
        functools.partial(_out_body, alpha=alpha),
        grid=(n // tm,),
        in_specs=[row(width)] * 4 + [full(w_out), row(d), full(g), full(b)],
        out_specs=[row(d), row(d)],
        out_shape=[jax.ShapeDtypeStruct((n, d), F32), jax.ShapeDtypeStruct((n, d), BF16)],
        compiler_params=_params(("parallel",), 56),
        name="out_proj_ln",
    )(*ys, w_out, x, g, b)


def _ffn_body(xb_ref, wu_ref, wd_ref, x_ref, g_ref, b_ref, o_ref, ob_ref, acc, *, alpha):
    f = pl.program_id(1)

    @pl.when(f == 0)
    def _():
        acc[...] = alpha * x_ref[...]

    hidden = jnp.maximum(_dot(xb_ref[...], wu_ref[...]), 0.0)
    acc[...] += _dot((hidden * hidden).astype(BF16), wd_ref[...])

    @pl.when(f == pl.num_programs(1) - 1)
    def _():
        y = _layer_norm(acc[...], g_ref[...], b_ref[...])
        o_ref[...] = y
        ob_ref[...] = y.astype(BF16)


def _ffn(xb, x, w_up, w_down, g, b, alpha, tm=512, tf=1024):
    n, d = x.shape
    ff = w_up.shape[1]
    tm = min(tm, n)
    g = g.astype(F32).reshape(1, d)
    b = b.astype(F32).reshape(1, d)
    row = pl.BlockSpec((tm, d), lambda i, f: (i, 0))
    vec = pl.BlockSpec((1, d), lambda i, f: (0, 0))
    return pl.pallas_call(
        functools.partial(_ffn_body, alpha=alpha),
        grid=(n // tm, ff // tf),
        in_specs=[row, pl.BlockSpec((d, tf), lambda i, f: (0, f)), pl.BlockSpec((tf, d), lambda i, f: (f, 0)),
                  row, vec, vec],
        out_specs=[row, row],
        out_shape=[jax.ShapeDtypeStruct((n, d), F32), jax.ShapeDtypeStruct((n, d), BF16)],
        scratch_shapes=[pltpu.VMEM((tm, d), F32)],
        compiler_params=_params(("parallel", "arbitrary"), 56),
        name="ffn_ln",
    )(xb, w_up, w_down, x, g, b)


def kernel(x, w_in, s5_lambda_re, s5_lambda_im, s5_log_step, s5_b_re, s5_b_im, s5_c_re, s5_c_im, s5_d, s5_glu_w, s5_glu_b, sgu_norm_g, sgu_norm_b, sgu_w, sgu_b, pool_w, pool_scale, dn_conv_w, dn_a_log, dn_dt_bias, dn_norm_g, w_out, ln1_g, ln1_b, w_up, w_down, ln2_g, ln2_b):
    bsz, seq, d = x.shape
    depth = w_in.shape[0]
    n = bsz * seq
    width = s5_glu_w.shape[1]
    heads = dn_a_log.shape[1]
    main_cols = 8 * width
    alpha = (2 * depth) ** 0.25
    s5_sub = min(256, seq)

    xf = x.reshape(n, d).astype(F32)
    xb = xf
    w_in_t = jnp.swapaxes(w_in, 1, 2)
    for i in range(depth):
        w_in_b = _cast_bf16(w_in_t, i, 3 * LANES, pad_rows_to=main_cols + LANES)
        proj, tail, w_up_b, w_down_b, w_out_b = _proj(xb, w_in_b, main_cols, (w_up, w_down, w_out), i)
        prep = _s5_prepare(s5_lambda_re[i], s5_lambda_im[i], s5_log_step[i], s5_b_re[i], s5_b_im[i],
                           s5_c_re[i], s5_c_im[i], s5_d[i], s5_sub // SUBLANES)
        y_s5 = _s5(proj, bsz, seq, prep, s5_glu_w[i], s5_glu_b[i], s5_sub)
        y_sgu = _sgu(proj, n, width, sgu_norm_g[i], sgu_norm_b[i], sgu_w[i], sgu_b[i])
        y_pool = _pool(proj, bsz, seq, width, pool_w[i], pool_scale[i], col_block=3)
        y_dn = _dn(proj, tail, bsz, seq, width, 4, dn_conv_w[i], dn_a_log[i], dn_dt_bias[i], dn_norm_g[i])
        xf, xb = _out_proj((y_s5, y_sgu, y_pool, y_dn), w_out_b, xf, ln1_g[i], ln1_b[i], alpha)
        xf, xb = _ffn(xb, xf, w_up_b, w_down_b, ln2_g[i], ln2_b[i], alpha)
    return xf.reshape(bsz, seq, d).astype(x.dtype)
```

```python
import functools
import math

import jax
import jax.numpy as jnp
from jax import lax
from jax.experimental import pallas as pl
from jax.experimental.pallas import tpu as pltpu

F32 = jnp.float32
BF16 = jnp.bfloat16
HIGHEST = lax.Precision.HIGHEST

LANES = 128
SUBLANES = 8

S5_CH_PER_GROUP = 16
S5_STATE = 64
S5_GROUPS_PER_SLAB = LANES // S5_CH_PER_GROUP
S5_SLAB_STATE = S5_GROUPS_PER_SLAB * S5_STATE
SGU_CHUNK = 128
SGU_HEAD_DIM = 64
POOL_WINDOWS = (2, 4, 8, 16)
POOL_HALO = 16
DN_HEAD_DIM = 128
DN_CONV = 4
DN_CHUNK = 64
DN_HALO = 8
LN_EPS = 1e-5
RMS_EPS = 1e-6
L2_EPS = 1e-6


def _params(semantics, vmem_mib):
    return pltpu.CompilerParams(dimension_semantics=semantics, vmem_limit_bytes=vmem_mib * 1024 * 1024)


def _layer_norm(y, g, b):
    mu = jnp.mean(y, axis=-1, keepdims=True)
    yc = y - mu
    var = jnp.mean(yc * yc, axis=-1, keepdims=True)
    return yc * lax.rsqrt(var + LN_EPS) * g + b


def _dot(a, b):
    return jnp.dot(a, b, preferred_element_type=F32)


def _dot_nt(a, b, precision=None):
    return lax.dot_general(a, b, (((1,), (1,)), ((), ())), precision=precision, preferred_element_type=F32)


def _dot_tn(a, b):
    return lax.dot_general(a, b, (((0,), (0,)), ((), ())), preferred_element_type=F32)


def _split3(x):
    hi = x.astype(BF16)
    rest = x - hi.astype(F32)
    mid = rest.astype(BF16)
    lo = (rest - mid.astype(F32)).astype(BF16)
    return hi, mid, lo


def _proj_body(x_ref, w_ref, wt_ref, *refs, n_cast):
    cast_in, (o_ref, ot_ref) = refs[:n_cast], refs[n_cast:n_cast + 2]
    cast_out, scratch = refs[n_cast + 2:2 * n_cast + 2], refs[2 * n_cast + 2:]
    if scratch:
        xb_ref, = scratch

        @pl.when(pl.program_id(1) == 0)
        def _():
            xb_ref[...] = x_ref[...].astype(BF16)
    else:
        xb_ref = x_ref
    o_ref[...] = _dot_nt(xb_ref[...], w_ref[...])
    for src, dst in zip(cast_in, cast_out):
        dst[...] = src[...].astype(BF16)

    @pl.when(pl.program_id(1) == 0)
    def _():
        ot_ref[...] = _dot_nt(xb_ref[...], wt_ref[...])


def _proj(x, w_t, cols, stacked, layer, tm=1024, tn=1024):
    n, d = x.shape
    tm = min(tm, n)
    ni, nj = n // tm, cols // tn
    scratch = [] if x.dtype == BF16 else [pltpu.VMEM((tm, d), BF16)]
    cast_rows = [w.shape[1] // (ni * nj) for w in stacked]
    cast_in = [pl.BlockSpec((None, r, w.shape[2]), lambda i, j: (layer, i * nj + j, 0))
               for w, r in zip(stacked, cast_rows)]
    cast_out = [pl.BlockSpec((r, w.shape[2]), lambda i, j: (i * nj + j, 0)) for w, r in zip(stacked, cast_rows)]
    return pl.pallas_call(
        functools.partial(_proj_body, n_cast=len(stacked)),
        grid=(ni, nj),
        in_specs=[pl.BlockSpec((tm, d), lambda i, j: (i, 0)),
                  pl.BlockSpec((tn, d), lambda i, j: (j, 0)),
                  pl.BlockSpec((LANES, d), lambda i, j: (cols // LANES, 0))] + cast_in,
        out_specs=[pl.BlockSpec((tm, tn), lambda i, j: (i, j)),
                   pl.BlockSpec((tm, LANES), lambda i, j: (i, 0))] + cast_out,
        out_shape=[jax.ShapeDtypeStruct((n, cols), F32), jax.ShapeDtypeStruct((n, LANES), F32)]
        + [jax.ShapeDtypeStruct(w.shape[1:], BF16) for w in stacked],
        scratch_shapes=scratch,
        compiler_params=_params(("parallel", "arbitrary"), 56),
        name="proj",
    )(x, w_t, w_t, *stacked)


def _cast_body(w_ref, o_ref, *, valid_rows):
    w = w_ref[...]
    if valid_rows % w.shape[0]:
        row = pl.program_id(0) * w.shape[0] + lax.broadcasted_iota(jnp.int32, w.shape, 0)
        w = jnp.where(row < valid_rows, w, 0.0)
    o_ref[...] = w.astype(o_ref.dtype)


def _cast_bf16(w, layer, block_rows, pad_rows_to=None):
    _, r, c = w.shape
    out_r = pad_rows_to or r
    return pl.pallas_call(
        functools.partial(_cast_body, valid_rows=r),
        grid=(out_r // block_rows,),
        in_specs=[pl.BlockSpec((None, block_rows, c), lambda i: (layer, i, 0))],
        out_specs=pl.BlockSpec((block_rows, c), lambda i: (i, 0)),
        out_shape=jax.ShapeDtypeStruct((out_r, c), BF16),
        compiler_params=_params(("parallel",), 40),
        name="cast_bf16",
    )(w)


def _s5_body(u_ref, perm_ref, unperm_ref, wb_ref, wc_ref, are_ref, aim_ref, pre_ref, pim_ref, d_ref, gw_ref, gb_ref,
             cast_ref, o_ref, cast_out_ref, sre, sim, hre, him, y_scr, *, sub, steps):
    cast_out_ref[...] = cast_ref[...].astype(BF16)
    n_slabs = u_ref.shape[1] // LANES
    n_sub = u_ref.shape[0] // sub
    sw = S5_SLAB_STATE
    tiles = sw // LANES

    @pl.when(pl.program_id(1) == 0)
    def _():
        hre[...] = jnp.zeros_like(hre)
        him[...] = jnp.zeros_like(him)

    perm = perm_ref[...]
    us = []
    for a in range(n_sub):
        u1, u2, u3 = _split3(u_ref[a * sub:(a + 1) * sub, :])
        u = _dot(perm, u1) + _dot(perm, u2) + _dot(perm, u3)
        us.append(u)
        ub = u.astype(BF16)
        for j in range(n_slabs):
            bu = _dot(ub[:, j * LANES:(j + 1) * LANES], wb_ref[j])
            for k in range(tiles):
                sre[a, j * tiles + k] = bu[:, k * LANES:(k + 1) * LANES]
                sim[a, j * tiles + k] = bu[:, sw + k * LANES:sw + (k + 1) * LANES]

    for a in range(n_sub):
        out_rows = slice(a * sub, (a + 1) * sub)
        for j in range(n_slabs):
            ts = [j * tiles + k for k in range(tiles)]
            cols = [slice(t * LANES, (t + 1) * LANES) for t in ts]

            ar = [jnp.broadcast_to(are_ref[:, c], (SUBLANES, LANES)) for c in cols]
            ai = [jnp.broadcast_to(aim_ref[:, c], (SUBLANES, LANES)) for c in cols]
            hr = [jnp.zeros((SUBLANES, LANES), F32)] * tiles
            hi = [jnp.zeros((SUBLANES, LANES), F32)] * tiles
            for r in range(steps):
                rows = slice(r * SUBLANES, (r + 1) * SUBLANES)
                for k, t in enumerate(ts):
                    hr[k], hi[k] = (ar[k] * hr[k] - ai[k] * hi[k] + sre[a, t, rows, :],
                                    ar[k] * hi[k] + ai[k] * hr[k] + sim[a, t, rows, :])
                    sre[a, t, rows, :] = hr[k]
                    sim[a, t, rows, :] = hi[k]

            cr8, ci8 = [], []
            for k in range(tiles):
                apr = pre_ref[steps - 1:steps, cols[k]]
                api = pim_ref[steps - 1:steps, cols[k]]
                cr = hre[:, cols[k]]
                ci = him[:, cols[k]]
                crs, cis = [], []
                for s in range(SUBLANES):
                    crs.append(cr)
                    cis.append(ci)
                    cr, ci = (apr * cr - api * ci + hr[k][s:s + 1, :], apr * ci + api * cr + hi[k][s:s + 1, :])
                hre[:, cols[k]] = cr
                him[:, cols[k]] = ci
                cr8.append(jnp.concatenate(crs, axis=0))
                ci8.append(jnp.concatenate(cis, axis=0))
            for r in range(steps):
                rows = slice(r * SUBLANES, (r + 1) * SUBLANES)
                for k, t in enumerate(ts):
                    pr = jnp.broadcast_to(pre_ref[r:r + 1, cols[k]], (SUBLANES, LANES))
                    pi = jnp.broadcast_to(pim_ref[r:r + 1, cols[k]], (SUBLANES, LANES))
                    sre[a, t, rows, :] = sre[a, t, rows, :] + (pr * cr8[k] - pi * ci8[k])
                    sim[a, t, rows, :] = sim[a, t, rows, :] + (pr * ci8[k] + pi * cr8[k])

            h_re = jnp.concatenate([sre[a, t] for t in ts], axis=1).astype(BF16)
            h_im = jnp.concatenate([sim[a, t] for t in ts], axis=1).astype(BF16)
            y = _dot(h_re, wc_ref[j, :sw, :]) + _dot(h_im, wc_ref[j, sw:, :])
            lc = slice(j * LANES, (j + 1) * LANES)
            y = y + d_ref[:, lc] * us[a][:, lc]
            y_scr[out_rows, lc] = jax.nn.gelu(y)

        y = y_scr[out_rows, :]
        gate = _dot(y.astype(BF16), gw_ref[...]) + gb_ref[...]
        out = (y * jax.nn.sigmoid(gate)).astype(BF16)
        o_ref[out_rows, :] = _dot(unperm_ref[...], out).astype(o_ref.dtype)


def _s5_prepare(lam_re, lam_im, log_step, b_re, b_im, c_re, c_im, d, steps):
    g, p = lam_re.shape
    h = S5_CH_PER_GROUP
    q = S5_GROUPS_PER_SLAB
    n_slabs = g // q
    step = jnp.exp(log_step.astype(F32))[:, None]
    lr, li = lam_re.astype(F32), lam_im.astype(F32)
    mag = jnp.exp(lr * step)
    a_re, a_im = mag * jnp.cos(li * step), mag * jnp.sin(li * step)
    den = lr * lr + li * li
    f_re = ((a_re - 1.0) * lr + a_im * li) / den
    f_im = (a_im * lr - (a_re - 1.0) * li) / den
    bb_re = f_re[:, :, None] * b_re - f_im[:, :, None] * b_im
    bb_im = f_re[:, :, None] * b_im + f_im[:, :, None] * b_re
    eye = jnp.eye(q, dtype=F32)

    def in_block(m):
        m = m.reshape(n_slabs, q, p, h)
        return jnp.einsum('jqph,qr->jqhrp', m, eye).reshape(n_slabs, q * h, q * p)

    def out_block(m):
        m = m.reshape(n_slabs, q, h, p)
        return jnp.einsum('jqhp,qr->jqprh', m, eye).reshape(n_slabs, q * p, q * h)

    wb = jnp.concatenate([in_block(bb_re), in_block(bb_im)], axis=2).astype(BF16)
    wc = jnp.concatenate([out_block(c_re.astype(F32)), out_block(-c_im.astype(F32))], axis=1).astype(BF16)
    k = jnp.arange(1, steps + 1, dtype=F32)[:, None, None]
    pmag = jnp.exp(lr * step * k)
    ang = li * step * k
    pw_re = (pmag * jnp.cos(ang)).reshape(steps, g * p)
    pw_im = (pmag * jnp.sin(ang)).reshape(steps, g * p)
    return (wb, wc, a_re.reshape(1, g * p), a_im.reshape(1, g * p), pw_re, pw_im,
            d.astype(F32).reshape(1, g * h))


def _s5(proj, bsz, seq, prep, glu_w, glu_b, sub, cast_w, layer, n_sub=2):
    wb, wc, a_re, a_im, pw_re, pw_im, d = prep
    tl = min(sub * n_sub, seq)
    steps = sub // SUBLANES
    width = d.shape[1]
    states = pw_re.shape[1]
    nl = seq // tl
    full = lambda a: pl.BlockSpec(a.shape, lambda b, l: (0,) * a.ndim)
    gw = glu_w.astype(BF16)
    gb = glu_b.astype(F32).reshape(1, width)
    dest = jnp.arange(sub)
    src = (dest % SUBLANES) * steps + dest // SUBLANES
    perm = (src[:, None] == jnp.arange(sub)[None, :]).astype(BF16)
    unperm = perm.T
    state_scratch = pltpu.VMEM((tl // sub, states // LANES, sub, LANES), F32)
    cast_rows = cast_w.shape[1] // (bsz * nl)
    return pl.pallas_call(
        functools.partial(_s5_body, sub=sub, steps=steps),
        grid=(bsz, nl),
        in_specs=[pl.BlockSpec((tl, width), lambda b, l: (b * nl + l, 0)), full(perm), full(unperm),
                  full(wb), full(wc), full(a_re), full(a_im), full(pw_re), full(pw_im), full(d), full(gw), full(gb),
                  pl.BlockSpec((None, cast_rows, cast_w.shape[2]), lambda b, l: (layer, b * nl + l, 0))],
        out_specs=[pl.BlockSpec((tl, width), lambda b, l: (b * nl + l, 0)),
                   pl.BlockSpec((cast_rows, cast_w.shape[2]), lambda b, l: (b * nl + l, 0))],
        out_shape=[jax.ShapeDtypeStruct((bsz * seq, width), BF16), jax.ShapeDtypeStruct(cast_w.shape[1:], BF16)],
        scratch_shapes=[state_scratch, state_scratch,
                        pltpu.VMEM((1, states), F32), pltpu.VMEM((1, states), F32),
                        pltpu.VMEM((tl, width), F32)],
        compiler_params=_params(("arbitrary", "arbitrary"), 40),
        name="s5",
    )(proj, perm, unperm, wb, wc, a_re, a_im, pw_re, pw_im, d, gw, gb, cast_w)


def _sgu_body(zu_ref, zv_ref, g_ref, b_ref, w_ref, bias_ref, o_ref, *, tl):
    width = zu_ref.shape[1]
    u = jax.nn.gelu(zu_ref[...])
    v = _layer_norm(jax.nn.gelu(zv_ref[...]), g_ref[...], b_ref[...])
    vb = v.astype(BF16)
    row = lax.broadcasted_iota(jnp.int32, (2 * SGU_CHUNK, SGU_CHUNK), 0)
    col = lax.broadcasted_iota(jnp.int32, (2 * SGU_CHUNK, SGU_CHUNK), 1)
    causal = col <= (row % SGU_CHUNK)
    lane = lax.broadcasted_iota(jnp.int32, (SGU_CHUNK, LANES), 1)
    first = lane < SGU_HEAD_DIM
    for j in range(width // LANES):
        w = jnp.where(causal, w_ref[j], 0.0).astype(BF16)
        bias = bias_ref[j]
        for c in range(tl // SGU_CHUNK):
            rs = slice(c * SGU_CHUNK, (c + 1) * SGU_CHUNK)
            cs = slice(j * LANES, (j + 1) * LANES)
            r = _dot(w, vb[rs, cs])
            mixed = jnp.where(first, r[:SGU_CHUNK], r[SGU_CHUNK:]) + bias
            o_ref[rs, cs] = (u[rs, cs] * mixed).astype(o_ref.dtype)


def _sgu(proj, n, width, norm_g, norm_b, w_s, b_s, tl=512):
    heads = w_s.shape[0]
    tl = min(tl, n)
    w_pairs = w_s.astype(F32).reshape(heads // 2, 2 * SGU_CHUNK, SGU_CHUNK)
    bias = jnp.repeat(b_s.astype(F32).T, SGU_HEAD_DIM, axis=1)
    bias = bias.reshape(SGU_CHUNK, heads // 2, LANES).transpose(1, 0, 2)
    g = norm_g.astype(F32).reshape(1, width)
    b = norm_b.astype(F32).reshape(1, width)
    full = lambda a: pl.BlockSpec(a.shape, lambda i: (0,) * a.ndim)
    return pl.pallas_call(
        functools.partial(_sgu_body, tl=tl),
        grid=(n // tl,),
        in_specs=[pl.BlockSpec((tl, width), lambda i: (i, 1)),
                  pl.BlockSpec((tl, width), lambda i: (i, 2)),
                  full(g), full(b), full(w_pairs), full(bias)],
        out_specs=pl.BlockSpec((tl, width), lambda i: (i, 0)),
        out_shape=jax.ShapeDtypeStruct((n, width), BF16),
        compiler_params=_params(("parallel",), 40),
        name="sgu",
    )(proj, proj, g, b, w_pairs, bias)


def _pool_body(p_ref, w_ref, sc_ref, o_ref, ext, *, tl):
    l = pl.program_id(1)

    @pl.when(l == 0)
    def _():
        ext[0:POOL_HALO, :] = jnp.zeros((POOL_HALO, ext.shape[1]), F32)

    @pl.when(l > 0)
    def _():
        ext[0:POOL_HALO, :] = ext[tl:tl + POOL_HALO, :]

    ext[POOL_HALO:POOL_HALO + tl, :] = p_ref[...]
    pos = l * tl + lax.broadcasted_iota(jnp.int32, (tl, LANES), 0)
    for gi, win in enumerate(POOL_WINDOWS):
        cs = slice(gi * LANES, (gi + 1) * LANES)
        x = ext[POOL_HALO:POOL_HALO + tl, cs]
        acc = ext[:, cs]
        span = 1
        while span < win:
            acc = acc + pltpu.roll(acc, span, 0)
            span *= 2
        acc = acc[POOL_HALO:, :]
        count = jnp.minimum(pos + 1, win).astype(F32)
        pooled = acc / count - x
        y = _dot(pooled.astype(BF16), w_ref[gi]) * sc_ref[:, cs]
        o_ref[:, cs] = y.astype(o_ref.dtype)


def _pool(proj, bsz, seq, width, w_pool, scale, col_block, tl=512):
    tl = min(tl, seq)
    nl = seq // tl
    w = w_pool.astype(BF16)
    sc = scale.astype(F32).reshape(1, width)
    return pl.pallas_call(
        functools.partial(_pool_body, tl=tl),
        grid=(bsz, nl),
        in_specs=[pl.BlockSpec((tl, width), lambda b, l: (b * nl + l, col_block)),
                  pl.BlockSpec(w.shape, lambda b, l: (0, 0, 0)),
                  pl.BlockSpec(sc.shape, lambda b, l: (0, 0))],
        out_specs=pl.BlockSpec((tl, width), lambda b, l: (b * nl + l, 0)),
        out_shape=jax.ShapeDtypeStruct((bsz * seq, width), BF16),
        scratch_shapes=[pltpu.VMEM((tl + POOL_HALO, width), F32)],
        compiler_params=_params(("arbitrary", "arbitrary"), 40),
        name="pool",
    )(proj, w, sc)


def _dn_prep_body(q_ref, k_ref, v_ref, tail_ref, cw_ref, alog_ref, dtb_ref, cast_ref,
                  vk_ref, qg_ref, kt_ref, qk_ref, low_ref, gl_ref, cast_out_ref, ext, *, tl, heads):
    cast_out_ref[...] = cast_ref[...].astype(BF16)
    l = pl.program_id(1)
    width = heads * DN_HEAD_DIM
    c = DN_CHUNK

    @pl.when(l == 0)
    def _():
        ext[:, 0:DN_HALO, :] = jnp.zeros((3, DN_HALO, width), F32)

    @pl.when(l > 0)
    def _():
        ext[:, 0:DN_HALO, :] = ext[:, tl:tl + DN_HALO, :]

    mixed = []
    for i, ref in enumerate((q_ref, k_ref, v_ref)):
        ext[i, DN_HALO:DN_HALO + tl, :] = ref[...]
        e = ext[i]
        acc = e * cw_ref[0:1, i * width:(i + 1) * width]
        for j in range(1, DN_CONV):
            acc = pltpu.roll(acc, 1, 0) + e * cw_ref[j:j + 1, i * width:(i + 1) * width]
        acc = acc[DN_HALO:, :]
        mixed.append(acc * jax.nn.sigmoid(acc))
    qc, kc, vc = mixed

    tail = tail_ref[...]
    sp_in = tail + dtb_ref[...]
    softplus = jnp.maximum(sp_in, 0.0) + jnp.log1p(jnp.exp(-jnp.abs(sp_in)))
    g_all = -jnp.exp(alog_ref[...]) * softplus
    beta_all = jax.nn.sigmoid(tail)

    ri = lax.broadcasted_iota(jnp.int32, (c, c), 0)
    ci = lax.broadcasted_iota(jnp.int32, (c, c), 1)
    causal = ri >= ci
    strict = ri > ci
    tri = causal.astype(BF16)
    sel = (lax.broadcasted_iota(jnp.int32, (heads * c, LANES), 0) // c
           == lax.broadcasted_iota(jnp.int32, (heads * c, LANES), 1)).astype(BF16)
    chunks = range(tl // c)

    gc = []
    for n in chunks:
        g1, g2, g3 = _split3(g_all[n * c:(n + 1) * c, :])
        gc.append(_dot(tri, g1) + _dot(tri, g2) + _dot(tri, g3))
    grow = []
    for n in chunks:
        g1, g2, g3 = _split3(gc[n])
        grow.append(_dot_nt(sel, g1) + _dot_nt(sel, g2) + _dot_nt(sel, g3))
    for n in chunks:
        rs = slice(n * c, (n + 1) * c)
        qn, kn, kbeta, kq, kkt = {}, {}, {}, {}, {}
        for h in range(heads):
            hs = slice(h * DN_HEAD_DIM, (h + 1) * DN_HEAD_DIM)
            q = qc[rs, hs]
            k = kc[rs, hs]
            q = q * lax.rsqrt(jnp.sum(q * q, axis=-1, keepdims=True) + L2_EPS) * (DN_HEAD_DIM ** -0.5)
            k = k * lax.rsqrt(jnp.sum(k * k, axis=-1, keepdims=True) + L2_EPS)
            kb = k * beta_all[rs, heads + h:heads + h + 1]
            qn[n, h], kn[n, h], kbeta[n, h] = q, k, kb
            if h % 2 == 0:
                kq[n, h] = _dot_nt(jnp.concatenate([kb.astype(BF16), q.astype(BF16)], axis=0), k.astype(BF16))
            else:
                kq[n, h] = _dot_nt(q.astype(BF16), k.astype(BF16))
                kkt[n, h] = _dot_nt(k.astype(BF16), kb.astype(BF16))
        for h in range(heads):
            hs = slice(h * DN_HEAD_DIM, (h + 1) * DN_HEAD_DIM)
            ms = slice(h * c, (h + 1) * c)
            gcol = gc[n][:, h:h + 1]
            glast = gc[n][c - 1:c, h:h + 1]
            decay = jnp.exp(-jnp.abs(gcol - grow[n][ms, :]))
            egc = jnp.exp(gcol)
            if h % 2 == 0:
                packed = jnp.where(strict, kq[n, h][:c] * decay, 0.0)
                qk_ref[rs, ms] = jnp.where(causal, kq[n, h][c:] * decay, 0.0)
            else:
                low_ref[rs, (h // 2) * c:(h // 2 + 1) * c] = jnp.where(ri < ci, kkt[n, h] * decay, packed)
                qk_ref[rs, ms] = jnp.where(causal, kq[n, h] * decay, 0.0)
            beta = beta_all[rs, heads + h:heads + h + 1]
            vk_ref[rs, 2 * h * DN_HEAD_DIM:(2 * h + 1) * DN_HEAD_DIM] = (vc[rs, hs] * beta).astype(BF16)
            vk_ref[rs, (2 * h + 1) * DN_HEAD_DIM:(2 * h + 2) * DN_HEAD_DIM] = (kbeta[n, h] * egc).astype(BF16)
            qg_ref[rs, hs] = (qn[n, h] * egc).astype(BF16)
            kt_ref[rs, hs] = (kn[n, h] * jnp.exp(glast - gcol)).astype(BF16)
            gl_ref[n * heads + h:n * heads + h + 1, :] = jnp.broadcast_to(jnp.exp(glast), (1, LANES))


def _tri_inv_body(l_ref, t_ref, u_scr):
    c = l_ref.shape[0]
    groups = c // SUBLANES
    sub = lax.broadcasted_iota(jnp.int32, (SUBLANES, LANES), 0)
    zero = jnp.zeros((SUBLANES, LANES), F32)
    grp = lambda cg: slice(cg * SUBLANES, (cg + 1) * SUBLANES)

    for i in range(c):
        diag = i // SUBLANES
        acc = [zero] * (diag + 1)
        acc[diag] = (sub == i % SUBLANES).astype(F32)
        for j in range(i):
            lij = jnp.broadcast_to(l_ref[i, j:j + 1, :], (SUBLANES, LANES))
            for cg in range(j // SUBLANES + 1):
                acc[cg] = acc[cg] - lij * t_ref[j, grp(cg), :]
        for cg in range(diag + 1):
            t_ref[i, grp(cg), :] = acc[cg]

    for i in reversed(range(c)):
        diag = i // SUBLANES
        acc = {cg: zero for cg in range(diag, groups)}
        acc[diag] = (sub == i % SUBLANES).astype(F32)
        for j in range(i + 1, c):
            lij = jnp.broadcast_to(l_ref[i, j:j + 1, :], (SUBLANES, LANES))
            for cg in range(j // SUBLANES, groups):
                acc[cg] = acc[cg] - lij * u_scr[j, grp(cg), :]
        for cg in range(diag, groups):
            u_scr[i, grp(cg), :] = acc[cg]

    for i in range(c):
        diag = i // SUBLANES
        for cg in range(diag + 1, groups):
            t_ref[i, grp(cg), :] = u_scr[i, grp(cg), :]
        t_ref[i, grp(diag), :] = jnp.where(sub < i % SUBLANES, t_ref[i, grp(diag), :],
                                           jnp.where(sub > i % SUBLANES, u_scr[i, grp(diag), :], 0.0))


def _dn_rec_body(t_ref, vk_ref, qg_ref, kt_ref, qk_ref, gl_ref, gate_ref, ng_ref, o_ref, state,
                 *, tl, heads):
    c = DN_CHUNK
    dk = DN_HEAD_DIM

    @pl.when(pl.program_id(1) == 0)
    def _():
        state[...] = jnp.zeros_like(state)

    chunks = range(tl // c)
    rows = lambda n: slice(n * c, (n + 1) * c)
    hcol = lambda h: slice(h * dk, (h + 1) * dk)
    mcol = lambda h: slice(h * c, (h + 1) * c)
    ri = lax.broadcasted_iota(jnp.int32, (c, c), 0)
    ci = lax.broadcasted_iota(jnp.int32, (c, c), 1)

    uw, uwb, kuw = {}, {}, {}
    for n in chunks:
        for h in range(heads):
            packed = t_ref[rows(n), mcol(h // 2)]
            vk = vk_ref[rows(n), 2 * h * dk:2 * (h + 1) * dk]
            if h % 2 == 0:
                uw[n, h] = _dot(jnp.where(ri > ci, packed, 0.0).astype(BF16), vk) + vk.astype(F32)
            else:
                uw[n, h] = _dot_tn(jnp.where(ri < ci, packed, 0.0).astype(BF16), vk) + vk.astype(F32)
            uwb[n, h] = uw[n, h].astype(BF16)
    for n in chunks:
        for h in range(heads):
            kuw[n, h] = _dot_tn(kt_ref[rows(n), hcol(h)], uwb[n, h])

    def emit(n, h, wq, v_bf16):
        o = wq[c:] + _dot(qk_ref[rows(n), mcol(h)].astype(BF16), v_bf16)
        o = o * lax.rsqrt(jnp.mean(o * o, axis=-1, keepdims=True) + RMS_EPS) * ng_ref[...]
        gt = gate_ref[rows(n), hcol(h)]
        o_ref[rows(n), hcol(h)] = (o * (gt * jax.nn.sigmoid(gt))).astype(o_ref.dtype)

    s = [state[h] for h in range(heads)]
    pending = []
    for n in chunks:
        sb = [s[h].astype(BF16) for h in range(heads)]
        for h in range(heads):
            s[h] = (s[h] * gl_ref[n * heads + h:n * heads + h + 1, :] + kuw[n, h][:, :dk]
                    - _dot(kuw[n, h][:, dk:].astype(BF16), sb[h]))
        wq = []
        for h in range(heads):
            lhs = jnp.concatenate([uwb[n, h][:, dk:], qg_ref[rows(n), hcol(h)]], axis=0)
            wq.append(_dot(lhs, sb[h]))
        for args in pending:
            emit(*args)
        pending = [(n, h, wq[h], (uw[n, h][:, :dk] - wq[h][:c]).astype(BF16)) for h in range(heads)]
    for args in pending:
        emit(*args)
    for h in range(heads):
        state[h] = s[h]


def _dn(proj, tail, bsz, seq, width, col_block, conv_w, a_log, dt_bias, norm_g, cast_w, layer, tl=512):
    heads = width // DN_HEAD_DIM
    c = DN_CHUNK
    n = bsz * seq
    tl = min(tl, seq)
    nl = seq // tl
    chunks = n // c
    cw = conv_w.astype(F32)
    pad = lambda a: jnp.pad(a.astype(F32), (0, LANES - heads)).reshape(1, LANES)
    alog = pad(a_log)
    dtb = pad(dt_bias)
    ng = norm_g.astype(F32).reshape(1, DN_HEAD_DIM)
    full = lambda a: pl.BlockSpec(a.shape, lambda b, l: (0,) * a.ndim)
    slab = lambda k: pl.BlockSpec((tl, width), lambda b, l: (b * nl + l, col_block + k))
    rows = lambda wd: pl.BlockSpec((tl, wd), lambda b, l: (b * nl + l, 0))
    gl_spec = pl.BlockSpec((tl // c * heads, LANES), lambda b, l: (b * nl + l, 0))
    wide = jax.ShapeDtypeStruct((n, width), BF16)
    mats = jax.ShapeDtypeStruct((n, heads * c), F32)
    pairs = heads // 2
    packed = jax.ShapeDtypeStruct((n, pairs * c), F32)
    cast_rows = cast_w.shape[1] // (bsz * nl)
    cast_in = pl.BlockSpec((None, cast_rows, cast_w.shape[2]), lambda b, l: (layer, b * nl + l, 0))
    cast_out = pl.BlockSpec((cast_rows, cast_w.shape[2]), lambda b, l: (b * nl + l, 0))
    vk, qg, kt, qk, low, gl, cast_b = pl.pallas_call(
        functools.partial(_dn_prep_body, tl=tl, heads=heads),
        grid=(bsz, nl),
        in_specs=[slab(0), slab(1), slab(2), rows(LANES), full(cw), full(alog), full(dtb), cast_in],
        out_specs=[rows(2 * width)] + [rows(width)] * 2 + [rows(heads * c), rows(pairs * c), gl_spec, cast_out],
        out_shape=[jax.ShapeDtypeStruct((n, 2 * width), BF16)] + [wide] * 2 + [mats, packed]
        + [jax.ShapeDtypeStruct((chunks * heads, LANES), F32), jax.ShapeDtypeStruct(cast_w.shape[1:], BF16)],
        scratch_shapes=[pltpu.VMEM((3, tl + DN_HALO, width), F32)],
        compiler_params=_params(("arbitrary", "arbitrary"), 40),
        name="deltanet_prep",
    )(proj, proj, proj, tail, cw, alog, dtb, cast_w)

    m_total = chunks * pairs
    low_t = low.reshape(chunks, c, pairs, c).transpose(1, 3, 0, 2).reshape(c, c, m_total)
    t_t = pl.pallas_call(
        _tri_inv_body,
        grid=(m_total // LANES,),
        in_specs=[pl.BlockSpec((c, c, LANES), lambda i: (0, 0, i))],
        out_specs=pl.BlockSpec((c, c, LANES), lambda i: (0, 0, i)),
        out_shape=jax.ShapeDtypeStruct((c, c, m_total), F32),
        scratch_shapes=[pltpu.VMEM((c, c, LANES), F32)],
        compiler_params=_params(("parallel",), 40),
        name="deltanet_tri_inv",
    )(low_t)
    t_mat = t_t.reshape(c, c, chunks, pairs).transpose(2, 0, 3, 1).reshape(n, pairs * c)

    y = pl.pallas_call(
        functools.partial(_dn_rec_body, tl=tl, heads=heads),
        grid=(bsz, nl),
        in_specs=[rows(pairs * c), rows(2 * width)] + [rows(width)] * 2
        + [rows(heads * c), gl_spec, slab(3), full(ng)],
        out_specs=rows(width),
        out_shape=wide,
        scratch_shapes=[pltpu.VMEM((heads, DN_HEAD_DIM, DN_HEAD_DIM), F32)],
        compiler_params=_params(("arbitrary", "arbitrary"), 40),
        name="deltanet_rec",
    )(t_mat, vk, qg, kt, qk, gl, proj, ng)
    return y, cast_b


def _out_body(y0_ref, y1_ref, y2_ref, y3_ref, w_ref, x_ref, g_ref, b_ref, o_ref, ob_ref, *, alpha):
    width = y0_ref.shape[1]
    half = x_ref.shape[0] // 2
    accs = []
    for rows in (slice(0, half), slice(half, 2 * half)):
        mixed = jnp.concatenate([ref[rows, :] for ref in (y0_ref, y1_ref, y2_ref, y3_ref)], axis=1)
        accs.append((rows, alpha * x_ref[rows, :] + _dot(mixed, w_ref[...])))
    for rows, acc in accs:
        y = _layer_norm(acc, g_ref[...], b_ref[...])
        o_ref[rows, :] = y
        ob_ref[rows, :] = y.astype(BF16)


def _out_proj(ys, w_out, x, g, b, alpha, tm=512):
    n, d = x.shape
    width = ys[0].shape[1]
    tm = min(tm, n)
    row = lambda wd: pl.BlockSpec((tm, wd), lambda i: (i, 0))
    full = lambda a: pl.BlockSpec(a.shape, lambda i: (0,) * a.ndim)
    g = g.astype(F32).reshape(1, d)
    b = b.astype(F32).reshape(1, d)
    return pl.pallas_call(
        functools.partial(_out_body, alpha=alpha),
        grid=(n // tm,),
        in_specs=[row(width)] * 4 + [full(w_out), row(d), full(g), full(b)],
        out_specs=[row(d), row(d)],
        out_shape=[jax.ShapeDtypeStruct((n, d), F32), jax.ShapeDtypeStruct((n, d), BF16)],
        compiler_params=_params(("parallel",), 56),
        name="out_proj_ln",
    )(*ys, w_out, x, g, b)


def _ffn_body(xb_ref, wu_ref, wd_ref, x_ref, g_ref, b_ref, o_ref, ob_ref, acc, *, alpha):
    f = pl.program_id(1)

    @pl.when(f == 0)
    def _():
        acc[...] = alpha * x_ref[...]

    hidden = jnp.maximum(_dot(xb_ref[...], wu_ref[...]), 0.0)
    acc[...] += _dot((hidden * hidden).astype(BF16), wd_ref[...])

    @pl.when(f == pl.num_programs(1) - 1)
    def _():
        y = _layer_norm(acc[...], g_ref[...], b_ref[...])
        o_ref[...] = y
        ob_ref[...] = y.astype(BF16)


def _ffn(xb, x, w_up, w_down, g, b, alpha, tm=512, tf=1024):
    n, d = x.shape
    ff = w_up.shape[1]
    tm = min(tm, n)
    g = g.astype(F32).reshape(1, d)
    b = b.astype(F32).reshape(1, d)
    row = pl.BlockSpec((tm, d), lambda i, f: (i, 0))
    vec = pl.BlockSpec((1, d), lambda i, f: (0, 0))
    return pl.pallas_call(
        functools.partial(_ffn_body, alpha=alpha),
        grid=(n // tm, ff // tf),
        in_specs=[row, pl.BlockSpec((d, tf), lambda i, f: (0, f)), pl.BlockSpec((tf, d), lambda i, f: (f, 0)),
                  row, vec, vec],
        out_specs=[row, row],
        out_shape=[jax.ShapeDtypeStruct((n, d), F32), jax.ShapeDtypeStruct((n, d), BF16)],
        scratch_shapes=[pltpu.VMEM((tm, d), F32)],
        compiler_params=_params(("parallel", "arbitrary"), 56),
        name="ffn_ln",
    )(xb, w_up, w_down, x, g, b)


def kernel(x, w_in, s5_lambda_re, s5_lambda_im, s5_log_step, s5_b_re, s5_b_im, s5_c_re, s5_c_im, s5_d, s5_glu_w, s5_glu_b, sgu_norm_g, sgu_norm_b, sgu_w, sgu_b, pool_w, pool_scale, dn_conv_w, dn_a_log, dn_dt_bias, dn_norm_g, w_out, ln1_g, ln1_b, w_up, w_down, ln2_g, ln2_b):
    bsz, seq, d = x.shape
    depth = w_in.shape[0]
    n = bsz * seq
    width = s5_glu_w.shape[1]
    heads = dn_a_log.shape[1]
    main_cols = 8 * width
    alpha = (2 * depth) ** 0.25
    s5_sub = min(256, seq)

    xf = x.reshape(n, d).astype(F32)
    xb = xf
    w_in_t = jnp.swapaxes(w_in, 1, 2)
    for i in range(depth):
        w_in_b = _cast_bf16(w_in_t, i, 3 * LANES, pad_rows_to=main_cols + LANES)
        proj, tail, w_out_b = _proj(xb, w_in_b, main_cols, (w_out,), i)
        prep = _s5_prepare(s5_lambda_re[i], s5_lambda_im[i], s5_log_step[i], s5_b_re[i], s5_b_im[i],
                           s5_c_re[i], s5_c_im[i], s5_d[i], s5_sub // SUBLANES)
        y_s5, w_up_b = _s5(proj, bsz, seq, prep, s5_glu_w[i], s5_glu_b[i], s5_sub, w_up, i)
        y_sgu = _sgu(proj, n, width, sgu_norm_g[i], sgu_norm_b[i], sgu_w[i], sgu_b[i])
        y_pool = _pool(proj, bsz, seq, width, pool_w[i], pool_scale[i], col_block=3)
        y_dn, w_down_b = _dn(proj, tail, bsz, seq, width, 4, dn_conv_w[i], dn_a_log[i], dn_dt_bias[i], dn_norm_g[i],
                             w_down, i)
        xf, xb = _out_proj((y_s5, y_sgu, y_pool, y_dn), w_out_b, xf, ln1_g[i], ln1_b[i], alpha)
        xf, xb = _ffn(xb, xf, w_up_b, w_down_b, ln2_g[i], ln2_b[i], alpha)
    return xf.reshape(bsz, seq, d).astype(x.dtype)
```

```python
import functools
import math

import jax
import jax.numpy as jnp
from jax import lax
from jax.experimental import pallas as pl
from jax.experimental.pallas import tpu as pltpu

F32 = jnp.float32
BF16 = jnp.bfloat16
HIGHEST = lax.Precision.HIGHEST

LANES = 128
SUBLANES = 8

S5_CH_PER_GROUP = 16
S5_STATE = 64
S5_GROUPS_PER_SLAB = LANES // S5_CH_PER_GROUP
S5_SLAB_STATE = S5_GROUPS_PER_SLAB * S5_STATE
SGU_CHUNK = 128
SGU_HEAD_DIM = 64
POOL_WINDOWS = (2, 4, 8, 16)
POOL_HALO = 16
DN_HEAD_DIM = 128
DN_CONV = 4
DN_CHUNK = 64
DN_HALO = 8
LN_EPS = 1e-5
RMS_EPS = 1e-6
L2_EPS = 1e-6


def _params(semantics, vmem_mib):
    return pltpu.CompilerParams(dimension_semantics=semantics, vmem_limit_bytes=vmem_mib * 1024 * 1024)


def _layer_norm(y, g, b):
    mu = jnp.mean(y, axis=-1, keepdims=True)
    yc = y - mu
    var = jnp.mean(yc * yc, axis=-1, keepdims=True)
    return yc * lax.rsqrt(var + LN_EPS) * g + b


def _dot(a, b):
    return jnp.dot(a, b, preferred_element_type=F32)


def _dot_nt(a, b, precision=None):
    return lax.dot_general(a, b, (((1,), (1,)), ((), ())), precision=precision, preferred_element_type=F32)


def _dot_tn(a, b):
    return lax.dot_general(a, b, (((0,), (0,)), ((), ())), preferred_element_type=F32)


def _split3(x):
    hi = x.astype(BF16)
    rest = x - hi.astype(F32)
    mid = rest.astype(BF16)
    lo = (rest - mid.astype(F32)).astype(BF16)
    return hi, mid, lo


def _proj_body(x_ref, w_ref, wt_ref, *refs, n_cast):
    cast_in, (o_ref, ot_ref) = refs[:n_cast], refs[n_cast:n_cast + 2]
    cast_out, scratch = refs[n_cast + 2:2 * n_cast + 2], refs[2 * n_cast + 2:]
    if scratch:
        xb_ref, = scratch

        @pl.when(pl.program_id(1) == 0)
        def _():
            xb_ref[...] = x_ref[...].astype(BF16)
    else:
        xb_ref = x_ref
    o_ref[...] = _dot_nt(xb_ref[...], w_ref[...])
    for src, dst in zip(cast_in, cast_out):
        dst[...] = src[...].astype(BF16)

    @pl.when(pl.program_id(1) == 0)
    def _():
        ot_ref[...] = _dot_nt(xb_ref[...], wt_ref[...])


def _proj(x, w_t, cols, stacked, layer, tm=1024, tn=1024):
    n, d = x.shape
    tm = min(tm, n)
    ni, nj = n // tm, cols // tn
    scratch = [] if x.dtype == BF16 else [pltpu.VMEM((tm, d), BF16)]
    cast_rows = [w.shape[1] // (ni * nj) for w in stacked]
    cast_in = [pl.BlockSpec((None, r, w.shape[2]), lambda i, j: (layer, i * nj + j, 0))
               for w, r in zip(stacked, cast_rows)]
    cast_out = [pl.BlockSpec((r, w.shape[2]), lambda i, j: (i * nj + j, 0)) for w, r in zip(stacked, cast_rows)]
    return pl.pallas_call(
        functools.partial(_proj_body, n_cast=len(stacked)),
        grid=(ni, nj),
        in_specs=[pl.BlockSpec((tm, d), lambda i, j: (i, 0)),
                  pl.BlockSpec((tn, d), lambda i, j: (j, 0)),
                  pl.BlockSpec((LANES, d), lambda i, j: (cols // LANES, 0))] + cast_in,
        out_specs=[pl.BlockSpec((tm, tn), lambda i, j: (i, j)),
                   pl.BlockSpec((tm, LANES), lambda i, j: (i, 0))] + cast_out,
        out_shape=[jax.ShapeDtypeStruct((n, cols), F32), jax.ShapeDtypeStruct((n, LANES), F32)]
        + [jax.ShapeDtypeStruct(w.shape[1:], BF16) for w in stacked],
        scratch_shapes=scratch,
        compiler_params=_params(("parallel", "arbitrary"), 56),
        name="proj",
    )(x, w_t, w_t, *stacked)


def _cast_body(w_ref, o_ref, *, valid_rows):
    w = w_ref[...]
    if valid_rows % w.shape[0]:
        row = pl.program_id(0) * w.shape[0] + lax.broadcasted_iota(jnp.int32, w.shape, 0)
        w = jnp.where(row < valid_rows, w, 0.0)
    o_ref[...] = w.astype(o_ref.dtype)


def _cast_bf16(w, layer, block_rows, pad_rows_to=None):
    _, r, c = w.shape
    out_r = pad_rows_to or r
    return pl.pallas_call(
        functools.partial(_cast_body, valid_rows=r),
        grid=(out_r // block_rows,),
        in_specs=[pl.BlockSpec((None, block_rows, c), lambda i: (layer, i, 0))],
        out_specs=pl.BlockSpec((block_rows, c), lambda i: (i, 0)),
        out_shape=jax.ShapeDtypeStruct((out_r, c), BF16),
        compiler_params=_params(("parallel",), 40),
        name="cast_bf16",
    )(w)


def _s5_body(u_ref, perm_ref, unperm_ref, wb_ref, wc_ref, are_ref, aim_ref, pre_ref, pim_ref, d_ref, gw_ref, gb_ref,
             cast_ref, o_ref, cast_out_ref, sre, sim, hre, him, y_scr, *, sub, steps):
    cast_out_ref[...] = cast_ref[...].astype(BF16)
    n_slabs = u_ref.shape[1] // LANES
    n_sub = u_ref.shape[0] // sub
    sw = S5_SLAB_STATE
    tiles = sw // LANES

    @pl.when(pl.program_id(1) == 0)
    def _():
        hre[...] = jnp.zeros_like(hre)
        him[...] = jnp.zeros_like(him)

    perm = perm_ref[...]
    us = []
    for a in range(n_sub):
        u1, u2, u3 = _split3(u_ref[a * sub:(a + 1) * sub, :])
        u = _dot(perm, u1) + _dot(perm, u2) + _dot(perm, u3)
        us.append(u)
        ub = u.astype(BF16)
        for j in range(n_slabs):
            bu = _dot(ub[:, j * LANES:(j + 1) * LANES], wb_ref[j])
            for k in range(tiles):
                sre[a, j * tiles + k] = bu[:, k * LANES:(k + 1) * LANES]
                sim[a, j * tiles + k] = bu[:, sw + k * LANES:sw + (k + 1) * LANES]

    for a in range(n_sub):
        out_rows = slice(a * sub, (a + 1) * sub)
        for j in range(n_slabs):
            ts = [j * tiles + k for k in range(tiles)]
            cols = [slice(t * LANES, (t + 1) * LANES) for t in ts]

            ar = [jnp.broadcast_to(are_ref[:, c], (SUBLANES, LANES)) for c in cols]
            ai = [jnp.broadcast_to(aim_ref[:, c], (SUBLANES, LANES)) for c in cols]
            hr = [jnp.zeros((SUBLANES, LANES), F32)] * tiles
            hi = [jnp.zeros((SUBLANES, LANES), F32)] * tiles
            for r in range(steps):
                rows = slice(r * SUBLANES, (r + 1) * SUBLANES)
                for k, t in enumerate(ts):
                    hr[k], hi[k] = (ar[k] * hr[k] - ai[k] * hi[k] + sre[a, t, rows, :],
                                    ar[k] * hi[k] + ai[k] * hr[k] + sim[a, t, rows, :])
                    sre[a, t, rows, :] = hr[k]
                    sim[a, t, rows, :] = hi[k]

            cr8, ci8 = [], []
            for k in range(tiles):
                apr = pre_ref[steps - 1:steps, cols[k]]
                api = pim_ref[steps - 1:steps, cols[k]]
                cr = hre[:, cols[k]]
                ci = him[:, cols[k]]
                crs, cis = [], []
                for s in range(SUBLANES):
                    crs.append(cr)
                    cis.append(ci)
                    cr, ci = (apr * cr - api * ci + hr[k][s:s + 1, :], apr * ci + api * cr + hi[k][s:s + 1, :])
                hre[:, cols[k]] = cr
                him[:, cols[k]] = ci
                cr8.append(jnp.concatenate(crs, axis=0))
                ci8.append(jnp.concatenate(cis, axis=0))
            for r in range(steps):
                rows = slice(r * SUBLANES, (r + 1) * SUBLANES)
                for k, t in enumerate(ts):
                    pr = jnp.broadcast_to(pre_ref[r:r + 1, cols[k]], (SUBLANES, LANES))
                    pi = jnp.broadcast_to(pim_ref[r:r + 1, cols[k]], (SUBLANES, LANES))
                    sre[a, t, rows, :] = sre[a, t, rows, :] + (pr * cr8[k] - pi * ci8[k])
                    sim[a, t, rows, :] = sim[a, t, rows, :] + (pr * ci8[k] + pi * cr8[k])

            h_re = jnp.concatenate([sre[a, t] for t in ts], axis=1).astype(BF16)
            h_im = jnp.concatenate([sim[a, t] for t in ts], axis=1).astype(BF16)
            y = _dot(h_re, wc_ref[j, :sw, :]) + _dot(h_im, wc_ref[j, sw:, :])
            lc = slice(j * LANES, (j + 1) * LANES)
            y = y + d_ref[:, lc] * us[a][:, lc]
            y_scr[out_rows, lc] = jax.nn.gelu(y)

        y = y_scr[out_rows, :]
        gate = _dot(y.astype(BF16), gw_ref[...]) + gb_ref[...]
        out = (y * jax.nn.sigmoid(gate)).astype(BF16)
        o_ref[out_rows, :] = _dot(unperm_ref[...], out).astype(o_ref.dtype)


def _s5_prepare(lam_re, lam_im, log_step, b_re, b_im, c_re, c_im, d, steps):
    g, p = lam_re.shape
    h = S5_CH_PER_GROUP
    q = S5_GROUPS_PER_SLAB
    n_slabs = g // q
    step = jnp.exp(log_step.astype(F32))[:, None]
    lr, li = lam_re.astype(F32), lam_im.astype(F32)
    mag = jnp.exp(lr * step)
    a_re, a_im = mag * jnp.cos(li * step), mag * jnp.sin(li * step)
    den = lr * lr + li * li
    f_re = ((a_re - 1.0) * lr + a_im * li) / den
    f_im = (a_im * lr - (a_re - 1.0) * li) / den
    bb_re = f_re[:, :, None] * b_re - f_im[:, :, None] * b_im
    bb_im = f_re[:, :, None] * b_im + f_im[:, :, None] * b_re
    eye = jnp.eye(q, dtype=F32)

    def in_block(m):
        m = m.reshape(n_slabs, q, p, h)
        return jnp.einsum('jqph,qr->jqhrp', m, eye).reshape(n_slabs, q * h, q * p)

    def out_block(m):
        m = m.reshape(n_slabs, q, h, p)
        return jnp.einsum('jqhp,qr->jqprh', m, eye).reshape(n_slabs, q * p, q * h)

    wb = jnp.concatenate([in_block(bb_re), in_block(bb_im)], axis=2).astype(BF16)
    wc = jnp.concatenate([out_block(c_re.astype(F32)), out_block(-c_im.astype(F32))], axis=1).astype(BF16)
    k = jnp.arange(1, steps + 1, dtype=F32)[:, None, None]
    pmag = jnp.exp(lr * step * k)
    ang = li * step * k
    pw_re = (pmag * jnp.cos(ang)).reshape(steps, g * p)
    pw_im = (pmag * jnp.sin(ang)).reshape(steps, g * p)
    return (wb, wc, a_re.reshape(1, g * p), a_im.reshape(1, g * p), pw_re, pw_im,
            d.astype(F32).reshape(1, g * h))


def _s5(proj, bsz, seq, prep, glu_w, glu_b, sub, cast_w, layer, n_sub=2):
    wb, wc, a_re, a_im, pw_re, pw_im, d = prep
    tl = min(sub * n_sub, seq)
    steps = sub // SUBLANES
    width = d.shape[1]
    states = pw_re.shape[1]
    nl = seq // tl
    full = lambda a: pl.BlockSpec(a.shape, lambda b, l: (0,) * a.ndim)
    gw = glu_w.astype(BF16)
    gb = glu_b.astype(F32).reshape(1, width)
    dest = jnp.arange(sub)
    src = (dest % SUBLANES) * steps + dest // SUBLANES
    perm = (src[:, None] == jnp.arange(sub)[None, :]).astype(BF16)
    unperm = perm.T
    state_scratch = pltpu.VMEM((tl // sub, states // LANES, sub, LANES), F32)
    cast_rows = cast_w.shape[1] // (bsz * nl)
    return pl.pallas_call(
        functools.partial(_s5_body, sub=sub, steps=steps),
        grid=(bsz, nl),
        in_specs=[pl.BlockSpec((tl, width), lambda b, l: (b * nl + l, 0)), full(perm), full(unperm),
                  full(wb), full(wc), full(a_re), full(a_im), full(pw_re), full(pw_im), full(d), full(gw), full(gb),
                  pl.BlockSpec((None, cast_rows, cast_w.shape[2]), lambda b, l: (layer, b * nl + l, 0))],
        out_specs=[pl.BlockSpec((tl, width), lambda b, l: (b * nl + l, 0)),
                   pl.BlockSpec((cast_rows, cast_w.shape[2]), lambda b, l: (b * nl + l, 0))],
        out_shape=[jax.ShapeDtypeStruct((bsz * seq, width), BF16), jax.ShapeDtypeStruct(cast_w.shape[1:], BF16)],
        scratch_shapes=[state_scratch, state_scratch,
                        pltpu.VMEM((1, states), F32), pltpu.VMEM((1, states), F32),
                        pltpu.VMEM((tl, width), F32)],
        compiler_params=_params(("arbitrary", "arbitrary"), 40),
        name="s5",
    )(proj, perm, unperm, wb, wc, a_re, a_im, pw_re, pw_im, d, gw, gb, cast_w)


def _sgu_body(zu_ref, zv_ref, g_ref, b_ref, w_ref, bias_ref, o_ref, *, tl):
    width = zu_ref.shape[1]
    u = jax.nn.gelu(zu_ref[...])
    v = _layer_norm(jax.nn.gelu(zv_ref[...]), g_ref[...], b_ref[...])
    vb = v.astype(BF16)
    row = lax.broadcasted_iota(jnp.int32, (2 * SGU_CHUNK, SGU_CHUNK), 0)
    col = lax.broadcasted_iota(jnp.int32, (2 * SGU_CHUNK, SGU_CHUNK), 1)
    causal = col <= (row % SGU_CHUNK)
    lane = lax.broadcasted_iota(jnp.int32, (SGU_CHUNK, LANES), 1)
    first = lane < SGU_HEAD_DIM
    for j in range(width // LANES):
        w = jnp.where(causal, w_ref[j], 0.0).astype(BF16)
        bias = bias_ref[j]
        for c in range(tl // SGU_CHUNK):
            rs = slice(c * SGU_CHUNK, (c + 1) * SGU_CHUNK)
            cs = slice(j * LANES, (j + 1) * LANES)
            r = _dot(w, vb[rs, cs])
            mixed = jnp.where(first, r[:SGU_CHUNK], r[SGU_CHUNK:]) + bias
            o_ref[rs, cs] = (u[rs, cs] * mixed).astype(o_ref.dtype)


def _sgu(proj, n, width, norm_g, norm_b, w_s, b_s, tl=512):
    heads = w_s.shape[0]
    tl = min(tl, n)
    w_pairs = w_s.astype(F32).reshape(heads // 2, 2 * SGU_CHUNK, SGU_CHUNK)
    bias = jnp.repeat(b_s.astype(F32).T, SGU_HEAD_DIM, axis=1)
    bias = bias.reshape(SGU_CHUNK, heads // 2, LANES).transpose(1, 0, 2)
    g = norm_g.astype(F32).reshape(1, width)
    b = norm_b.astype(F32).reshape(1, width)
    full = lambda a: pl.BlockSpec(a.shape, lambda i: (0,) * a.ndim)
    return pl.pallas_call(
        functools.partial(_sgu_body, tl=tl),
        grid=(n // tl,),
        in_specs=[pl.BlockSpec((tl, width), lambda i: (i, 1)),
                  pl.BlockSpec((tl, width), lambda i: (i, 2)),
                  full(g), full(b), full(w_pairs), full(bias)],
        out_specs=pl.BlockSpec((tl, width), lambda i: (i, 0)),
        out_shape=jax.ShapeDtypeStruct((n, width), BF16),
        compiler_params=_params(("parallel",), 40),
        name="sgu",
    )(proj, proj, g, b, w_pairs, bias)


def _pool_body(p_ref, w_ref, sc_ref, o_ref, ext, *, tl):
    l = pl.program_id(1)

    @pl.when(l == 0)
    def _():
        ext[0:POOL_HALO, :] = jnp.zeros((POOL_HALO, ext.shape[1]), F32)

    @pl.when(l > 0)
    def _():
        ext[0:POOL_HALO, :] = ext[tl:tl + POOL_HALO, :]

    ext[POOL_HALO:POOL_HALO + tl, :] = p_ref[...]
    pos = l * tl + lax.broadcasted_iota(jnp.int32, (tl, LANES), 0)
    for gi, win in enumerate(POOL_WINDOWS):
        cs = slice(gi * LANES, (gi + 1) * LANES)
        x = ext[POOL_HALO:POOL_HALO + tl, cs]
        acc = ext[:, cs]
        span = 1
        while span < win:
            acc = acc + pltpu.roll(acc, span, 0)
            span *= 2
        acc = acc[POOL_HALO:, :]
        count = jnp.minimum(pos + 1, win).astype(F32)
        pooled = acc / count - x
        y = _dot(pooled.astype(BF16), w_ref[gi]) * sc_ref[:, cs]
        o_ref[:, cs] = y.astype(o_ref.dtype)


def _pool(proj, bsz, seq, width, w_pool, scale, col_block, tl=512):
    tl = min(tl, seq)
    nl = seq // tl
    w = w_pool.astype(BF16)
    sc = scale.astype(F32).reshape(1, width)
    return pl.pallas_call(
        functools.partial(_pool_body, tl=tl),
        grid=(bsz, nl),
        in_specs=[pl.BlockSpec((tl, width), lambda b, l: (b * nl + l, col_block)),
                  pl.BlockSpec(w.shape, lambda b, l: (0, 0, 0)),
                  pl.BlockSpec(sc.shape, lambda b, l: (0, 0))],
        out_specs=pl.BlockSpec((tl, width), lambda b, l: (b * nl + l, 0)),
        out_shape=jax.ShapeDtypeStruct((bsz * seq, width), BF16),
        scratch_shapes=[pltpu.VMEM((tl + POOL_HALO, width), F32)],
        compiler_params=_params(("arbitrary", "arbitrary"), 40),
        name="pool",
    )(proj, w, sc)


def _dn_prep_body(q_ref, k_ref, v_ref, tail_ref, cw_ref, alog_ref, dtb_ref, cast_ref,
                  vk_ref, qg_ref, kt_ref, qk_ref, low_ref, gl_ref, cast_out_ref, ext, *, tl, heads):
    cast_out_ref[...] = cast_ref[...].astype(BF16)
    l = pl.program_id(1)
    width = heads * DN_HEAD_DIM
    c = DN_CHUNK

    @pl.when(l == 0)
    def _():
        ext[:, 0:DN_HALO, :] = jnp.zeros((3, DN_HALO, width), F32)

    @pl.when(l > 0)
    def _():
        ext[:, 0:DN_HALO, :] = ext[:, tl:tl + DN_HALO, :]

    mixed = []
    for i, ref in enumerate((q_ref, k_ref, v_ref)):
        ext[i, DN_HALO:DN_HALO + tl, :] = ref[...]
        e = ext[i]
        acc = e * cw_ref[0:1, i * width:(i + 1) * width]
        for j in range(1, DN_CONV):
            acc = pltpu.roll(acc, 1, 0) + e * cw_ref[j:j + 1, i * width:(i + 1) * width]
        acc = acc[DN_HALO:, :]
        mixed.append(acc * jax.nn.sigmoid(acc))
    qc, kc, vc = mixed

    tail = tail_ref[...]
    sp_in = tail + dtb_ref[...]
    softplus = jnp.maximum(sp_in, 0.0) + jnp.log1p(jnp.exp(-jnp.abs(sp_in)))
    g_all = -jnp.exp(alog_ref[...]) * softplus
    beta_all = jax.nn.sigmoid(tail)

    ri = lax.broadcasted_iota(jnp.int32, (c, c), 0)
    ci = lax.broadcasted_iota(jnp.int32, (c, c), 1)
    causal = ri >= ci
    strict = ri > ci
    tri = causal.astype(BF16)
    sel = (lax.broadcasted_iota(jnp.int32, (heads * c, LANES), 0) // c
           == lax.broadcasted_iota(jnp.int32, (heads * c, LANES), 1)).astype(BF16)
    chunks = range(tl // c)

    gc = []
    for n in chunks:
        g1, g2, g3 = _split3(g_all[n * c:(n + 1) * c, :])
        gc.append(_dot(tri, g1) + _dot(tri, g2) + _dot(tri, g3))
    grow = []
    for n in chunks:
        g1, g2, g3 = _split3(gc[n])
        grow.append(_dot_nt(sel, g1) + _dot_nt(sel, g2) + _dot_nt(sel, g3))
    for n in chunks:
        rs = slice(n * c, (n + 1) * c)
        qn, kn, kbeta, kq, kkt = {}, {}, {}, {}, {}
        for h in range(heads):
            hs = slice(h * DN_HEAD_DIM, (h + 1) * DN_HEAD_DIM)
            q = qc[rs, hs]
            k = kc[rs, hs]
            q = q * lax.rsqrt(jnp.sum(q * q, axis=-1, keepdims=True) + L2_EPS) * (DN_HEAD_DIM ** -0.5)
            k = k * lax.rsqrt(jnp.sum(k * k, axis=-1, keepdims=True) + L2_EPS)
            kb = k * beta_all[rs, heads + h:heads + h + 1]
            qn[n, h], kn[n, h], kbeta[n, h] = q, k, kb
            if h % 2 == 0:
                kq[n, h] = _dot_nt(jnp.concatenate([kb.astype(BF16), q.astype(BF16)], axis=0), k.astype(BF16))
            else:
                kq[n, h] = _dot_nt(q.astype(BF16), k.astype(BF16))
                kkt[n, h] = _dot_nt(k.astype(BF16), kb.astype(BF16))
        for h in range(heads):
            hs = slice(h * DN_HEAD_DIM, (h + 1) * DN_HEAD_DIM)
            ms = slice(h * c, (h + 1) * c)
            gcol = gc[n][:, h:h + 1]
            glast = gc[n][c - 1:c, h:h + 1]
            decay = jnp.exp(-jnp.abs(gcol - grow[n][ms, :]))
            egc = jnp.exp(gcol)
            if h % 2 == 0:
                packed = jnp.where(strict, kq[n, h][:c] * decay, 0.0)
                qk_ref[rs, ms] = jnp.where(causal, kq[n, h][c:] * decay, 0.0)
            else:
                low_ref[rs, (h // 2) * c:(h // 2 + 1) * c] = jnp.where(ri < ci, kkt[n, h] * decay, packed)
                qk_ref[rs, ms] = jnp.where(causal, kq[n, h] * decay, 0.0)
            beta = beta_all[rs, heads + h:heads + h + 1]
            vk_ref[rs, 2 * h * DN_HEAD_DIM:(2 * h + 1) * DN_HEAD_DIM] = (vc[rs, hs] * beta).astype(BF16)
            vk_ref[rs, (2 * h + 1) * DN_HEAD_DIM:(2 * h + 2) * DN_HEAD_DIM] = (kbeta[n, h] * egc).astype(BF16)
            qg_ref[rs, hs] = (qn[n, h] * egc).astype(BF16)
            kt_ref[rs, hs] = (kn[n, h] * jnp.exp(glast - gcol)).astype(BF16)
            gl_ref[n * heads + h:n * heads + h + 1, :] = jnp.broadcast_to(jnp.exp(glast), (1, LANES))


def _tri_inv_body(l_ref, t_ref, lo_scr, up_scr):
    c, _, lanes = l_ref.shape
    groups = c // SUBLANES
    sub = lax.broadcasted_iota(jnp.int32, (SUBLANES, lanes), 0)
    zero = jnp.zeros((SUBLANES, lanes), F32)
    grp = lambda g: slice(g * SUBLANES, (g + 1) * SUBLANES)
    pick = lambda v, r: jnp.broadcast_to(v[r:r + 1, :], (SUBLANES, lanes))

    for j in range(c):
        d = l_ref[j, grp(j // SUBLANES), :]
        lo_scr[j] = jnp.where(sub > j % SUBLANES, d, 0.0)
        up_scr[j] = jnp.where(sub < j % SUBLANES, d, 0.0)

    for col in range(c):
        dg, dr = col // SUBLANES, col % SUBLANES
        unit = (sub == dr).astype(F32)
        t = {g: zero for g in range(dg, groups)}
        t[dg] = unit
        for j in range(col, c - 1):
            jg = j // SUBLANES
            b = pick(t[jg], j % SUBLANES)
            t[jg] = t[jg] - lo_scr[j] * b
            for g in range(jg + 1, groups):
                t[g] = t[g] - l_ref[j, grp(g), :] * b
        u = {g: zero for g in range(dg + 1)}
        u[dg] = unit
        for j in range(col, 0, -1):
            jg = j // SUBLANES
            b = pick(u[jg], j % SUBLANES)
            u[jg] = u[jg] - up_scr[j] * b
            for g in range(jg):
                u[g] = u[g] - l_ref[j, grp(g), :] * b
        for g in range(groups):
            if g < dg:
                t_ref[col, grp(g), :] = u[g]
            elif g > dg:
                t_ref[col, grp(g), :] = t[g]
            else:
                t_ref[col, grp(g), :] = jnp.where(sub > dr, t[g], jnp.where(sub < dr, u[g], 0.0))


def _dn_rec_body(t_ref, vk_ref, qg_ref, kt_ref, qk_ref, gl_ref, gate_ref, ng_ref, o_ref, state,
                 *, tl, heads):
    c = DN_CHUNK
    dk = DN_HEAD_DIM

    @pl.when(pl.program_id(1) == 0)
    def _():
        state[...] = jnp.zeros_like(state)

    chunks = range(tl // c)
    rows = lambda n: slice(n * c, (n + 1) * c)
    hcol = lambda h: slice(h * dk, (h + 1) * dk)
    mcol = lambda h: slice(h * c, (h + 1) * c)
    ri = lax.broadcasted_iota(jnp.int32, (c, c), 0)
    ci = lax.broadcasted_iota(jnp.int32, (c, c), 1)

    uw, uwb, kuw = {}, {}, {}
    for n in chunks:
        for h in range(heads):
            packed = t_ref[rows(n), mcol(h // 2)]
            vk = vk_ref[rows(n), 2 * h * dk:2 * (h + 1) * dk]
            if h % 2 == 0:
                uw[n, h] = _dot(jnp.where(ri > ci, packed, 0.0).astype(BF16), vk) + vk.astype(F32)
            else:
                uw[n, h] = _dot_tn(jnp.where(ri < ci, packed, 0.0).astype(BF16), vk) + vk.astype(F32)
            uwb[n, h] = uw[n, h].astype(BF16)
    for n in chunks:
        for h in range(heads):
            kuw[n, h] = _dot_tn(kt_ref[rows(n), hcol(h)], uwb[n, h])

    def emit(n, h, wq, v_bf16):
        o = wq[c:] + _dot(qk_ref[rows(n), mcol(h)].astype(BF16), v_bf16)
        o = o * lax.rsqrt(jnp.mean(o * o, axis=-1, keepdims=True) + RMS_EPS) * ng_ref[...]
        gt = gate_ref[rows(n), hcol(h)]
        o_ref[rows(n), hcol(h)] = (o * (gt * jax.nn.sigmoid(gt))).astype(o_ref.dtype)

    s = [state[h] for h in range(heads)]
    pending = []
    for n in chunks:
        sb = [s[h].astype(BF16) for h in range(heads)]
        for h in range(heads):
            s[h] = (s[h] * gl_ref[n * heads + h:n * heads + h + 1, :] + kuw[n, h][:, :dk]
                    - _dot(kuw[n, h][:, dk:].astype(BF16), sb[h]))
        wq = []
        for h in range(heads):
            lhs = jnp.concatenate([uwb[n, h][:, dk:], qg_ref[rows(n), hcol(h)]], axis=0)
            wq.append(_dot(lhs, sb[h]))
        for args in pending:
            emit(*args)
        pending = [(n, h, wq[h], (uw[n, h][:, :dk] - wq[h][:c]).astype(BF16)) for h in range(heads)]
    for args in pending:
        emit(*args)
    for h in range(heads):
        state[h] = s[h]


def _dn(proj, tail, bsz, seq, width, col_block, conv_w, a_log, dt_bias, norm_g, cast_w, layer, tl=512):
    heads = width // DN_HEAD_DIM
    c = DN_CHUNK
    n = bsz * seq
    tl = min(tl, seq)
    nl = seq // tl
    chunks = n // c
    cw = conv_w.astype(F32)
    pad = lambda a: jnp.pad(a.astype(F32), (0, LANES - heads)).reshape(1, LANES)
    alog = pad(a_log)
    dtb = pad(dt_bias)
    ng = norm_g.astype(F32).reshape(1, DN_HEAD_DIM)
    full = lambda a: pl.BlockSpec(a.shape, lambda b, l: (0,) * a.ndim)
    slab = lambda k: pl.BlockSpec((tl, width), lambda b, l: (b * nl + l, col_block + k))
    rows = lambda wd: pl.BlockSpec((tl, wd), lambda b, l: (b * nl + l, 0))
    gl_spec = pl.BlockSpec((tl // c * heads, LANES), lambda b, l: (b * nl + l, 0))
    wide = jax.ShapeDtypeStruct((n, width), BF16)
    mats = jax.ShapeDtypeStruct((n, heads * c), F32)
    pairs = heads // 2
    packed = jax.ShapeDtypeStruct((n, pairs * c), F32)
    cast_rows = cast_w.shape[1] // (bsz * nl)
    cast_in = pl.BlockSpec((None, cast_rows, cast_w.shape[2]), lambda b, l: (layer, b * nl + l, 0))
    cast_out = pl.BlockSpec((cast_rows, cast_w.shape[2]), lambda b, l: (b * nl + l, 0))
    vk, qg, kt, qk, low, gl, cast_b = pl.pallas_call(
        functools.partial(_dn_prep_body, tl=tl, heads=heads),
        grid=(bsz, nl),
        in_specs=[slab(0), slab(1), slab(2), rows(LANES), full(cw), full(alog), full(dtb), cast_in],
        out_specs=[rows(2 * width)] + [rows(width)] * 2 + [rows(heads * c), rows(pairs * c), gl_spec, cast_out],
        out_shape=[jax.ShapeDtypeStruct((n, 2 * width), BF16)] + [wide] * 2 + [mats, packed]
        + [jax.ShapeDtypeStruct((chunks * heads, LANES), F32), jax.ShapeDtypeStruct(cast_w.shape[1:], BF16)],
        scratch_shapes=[pltpu.VMEM((3, tl + DN_HALO, width), F32)],
        compiler_params=_params(("arbitrary", "arbitrary"), 40),
        name="deltanet_prep",
    )(proj, proj, proj, tail, cw, alog, dtb, cast_w)

    lane_block = min(LANES, chunks)
    low_t = low.reshape(chunks, c, pairs * c).transpose(2, 1, 0)
    inv_spec = pl.BlockSpec((c, c, lane_block), lambda p, cb: (p, 0, cb))
    t_t = pl.pallas_call(
        _tri_inv_body,
        grid=(pairs, chunks // lane_block),
        in_specs=[inv_spec],
        out_specs=inv_spec,
        out_shape=jax.ShapeDtypeStruct((pairs * c, c, chunks), F32),
        scratch_shapes=[pltpu.VMEM((c, SUBLANES, lane_block), F32), pltpu.VMEM((c, SUBLANES, lane_block), F32)],
        compiler_params=_params(("parallel", "parallel"), 40),
        name="deltanet_tri_inv",
    )(low_t)
    t_mat = t_t.transpose(2, 1, 0).reshape(n, pairs * c)

    y = pl.pallas_call(
        functools.partial(_dn_rec_body, tl=tl, heads=heads),
        grid=(bsz, nl),
        in_specs=[rows(pairs * c), rows(2 * width)] + [rows(width)] * 2
        + [rows(heads * c), gl_spec, slab(3), full(ng)],
        out_specs=rows(width),
        out_shape=wide,
        scratch_shapes=[pltpu.VMEM((heads, DN_HEAD_DIM, DN_HEAD_DIM), F32)],
        compiler_params=_params(("arbitrary", "arbitrary"), 40),
        name="deltanet_rec",
    )(t_mat, vk, qg, kt, qk, gl, proj, ng)
    return y, cast_b


def _out_body(y0_ref, y1_ref, y2_ref, y3_ref, w_ref, x_ref, g_ref, b_ref, o_ref, ob_ref, *, alpha):
    width = y0_ref.shape[1]
    half = x_ref.shape[0] // 2
    accs = []
    for rows in (slice(0, half), slice(half, 2 * half)):
        mixed = jnp.concatenate([ref[rows, :] for ref in (y0_ref, y1_ref, y2_ref, y3_ref)], axis=1)
        accs.append((rows, alpha * x_ref[rows, :] + _dot(mixed, w_ref[...])))
    for rows, acc in accs:
        y = _layer_norm(acc, g_ref[...], b_ref[...])
        o_ref[rows, :] = y
        ob_ref[rows, :] = y.astype(BF16)


def _out_proj(ys, w_out, x, g, b, alpha, tm=512):
    n, d = x.shape
    width = ys[0].shape[1]
    tm = min(tm, n)
    row = lambda wd: pl.BlockSpec((tm, wd), lambda i: (i, 0))
    full = lambda a: pl.BlockSpec(a.shape, lambda i: (0,) * a.ndim)
    g = g.astype(F32).reshape(1, d)
    b = b.astype(F32).reshape(1, d)
    return pl.pallas_call(
        functools.partial(_out_body, alpha=alpha),
        grid=(n // tm,),
        in_specs=[row(width)] * 4 + [full(w_out), row(d), full(g), full(b)],
        out_specs=[row(d), row(d)],
        out_shape=[jax.ShapeDtypeStruct((n, d), F32), jax.ShapeDtypeStruct((n, d), BF16)],
        compiler_params=_params(("parallel",), 56),
        name="out_proj_ln",
    )(*ys, w_out, x, g, b)


def _ffn_body(xb_ref, wu_ref, wd_ref, x_ref, g_ref, b_ref, o_ref, ob_ref, acc, *, alpha):
    f = pl.program_id(1)

    @pl.when(f == 0)
    def _():
        acc[...] = alpha * x_ref[...]

    hidden = jnp.maximum(_dot(xb_ref[...], wu_ref[...]), 0.0)
    acc[...] += _dot((hidden * hidden).astype(BF16), wd_ref[...])

    @pl.when(f == pl.num_programs(1) - 1)
    def _():
        y = _layer_norm(acc[...], g_ref[...], b_ref[...])
        o_ref[...] = y
        ob_ref[...] = y.astype(BF16)


def _ffn(xb, x, w_up, w_down, g, b, alpha, tm=512, tf=1024):
    n, d = x.shape
    ff = w_up.shape[1]
    tm = min(tm, n)
    g = g.astype(F32).reshape(1, d)
    b = b.astype(F32).reshape(1, d)
    row = pl.BlockSpec((tm, d), lambda i, f: (i, 0))
    vec = pl.BlockSpec((1, d), lambda i, f: (0, 0))
    return pl.pallas_call(
        functools.partial(_ffn_body, alpha=alpha),
        grid=(n // tm, ff // tf),
        in_specs=[row, pl.BlockSpec((d, tf), lambda i, f: (0, f)), pl.BlockSpec((tf, d), lambda i, f: (f, 0)),
                  row, vec, vec],
        out_specs=[row, row],
        out_shape=[jax.ShapeDtypeStruct((n, d), F32), jax.ShapeDtypeStruct((n, d), BF16)],
        scratch_shapes=[pltpu.VMEM((tm, d), F32)],
        compiler_params=_params(("parallel", "arbitrary"), 56),
        name="ffn_ln",
    )(xb, w_up, w_down, x, g, b)


def kernel(x, w_in, s5_lambda_re, s5_lambda_im, s5_log_step, s5_b_re, s5_b_im, s5_c_re, s5_c_im, s5_d, s5_glu_w, s5_glu_b, sgu_norm_g, sgu_norm_b, sgu_w, sgu_b, pool_w, pool_scale, dn_conv_w, dn_a_log, dn_dt_bias, dn_norm_g, w_out, ln1_g, ln1_b, w_up, w_down, ln2_g, ln2_b):
    bsz, seq, d = x.shape
    depth = w_in.shape[0]
    n = bsz * seq
    width = s5_glu_w.shape[1]
    heads = dn_a_log.shape[1]
    main_cols = 8 * width
    alpha = (2 * depth) ** 0.25
    s5_sub = min(256, seq)

    xf = x.reshape(n, d).astype(F32)
    xb = xf
    w_in_t = jnp.swapaxes(w_in, 1, 2)
    for i in range(depth):
        w_in_b = _cast_bf16(w_in_t, i, 3 * LANES, pad_rows_to=main_cols + LANES)
        proj, tail, w_out_b = _proj(xb, w_in_b, main_cols, (w_out,), i)
        prep = _s5_prepare(s5_lambda_re[i], s5_lambda_im[i], s5_log_step[i], s5_b_re[i], s5_b_im[i],
                           s5_c_re[i], s5_c_im[i], s5_d[i], s5_sub // SUBLANES)
        y_s5, w_up_b = _s5(proj, bsz, seq, prep, s5_glu_w[i], s5_glu_b[i], s5_sub, w_up, i)
        y_sgu = _sgu(proj, n, width, sgu_norm_g[i], sgu_norm_b[i], sgu_w[i], sgu_b[i])
        y_pool = _pool(proj, bsz, seq, width, pool_w[i], pool_scale[i], col_block=3)
        y_dn, w_down_b = _dn(proj, tail, bsz, seq, width, 4, dn_conv_w[i], dn_a_log[i], dn_dt_bias[i], dn_norm_g[i],
                             w_down, i)
        xf, xb = _out_proj((y_s5, y_sgu, y_pool, y_dn), w_out_b, xf, ln1_g[i], ln1_b[i], alpha)
        xf, xb = _ffn(xb, xf, w_up_b, w_down_b, ln2_g[i], ln2_b[i], alpha)
    return xf.reshape(bsz, seq, d).astype(x.dtype)
```

```python
import functools

import jax
import jax.numpy as jnp
from jax import lax
from jax.experimental import pallas as pl
from jax.experimental.pallas import tpu as pltpu

F32 = jnp.float32
BF16 = jnp.bfloat16

LANES = 128
SUBLANES = 8

S5_CH_PER_GROUP = 16
S5_STATE = 64
S5_GROUPS_PER_SLAB = LANES // S5_CH_PER_GROUP
S5_SLAB_STATE = S5_GROUPS_PER_SLAB * S5_STATE
SGU_CHUNK = 128
SGU_HEAD_DIM = 64
POOL_WINDOWS = (2, 4, 8, 16)
POOL_HALO = 16
DN_HEAD_DIM = 128
DN_CONV = 4
DN_CHUNK = 64
DN_HALO = 8
LN_EPS = 1e-5
RMS_EPS = 1e-6
L2_EPS = 1e-6


def _params(semantics, vmem_mib):
    return pltpu.CompilerParams(dimension_semantics=semantics, vmem_limit_bytes=vmem_mib * 1024 * 1024)


def _layer_norm(y, g, b):
    mu = jnp.mean(y, axis=-1, keepdims=True)
    yc = y - mu
    var = jnp.mean(yc * yc, axis=-1, keepdims=True)
    return yc * lax.rsqrt(var + LN_EPS) * g + b


def _dot(a, b):
    return jnp.dot(a, b, preferred_element_type=F32)


def _dot_nt(a, b):
    return lax.dot_general(a, b, (((1,), (1,)), ((), ())), preferred_element_type=F32)


def _dot_tn(a, b):
    return lax.dot_general(a, b, (((0,), (0,)), ((), ())), preferred_element_type=F32)


def _split3(x):
    hi = x.astype(BF16)
    rest = x - hi.astype(F32)
    mid = rest.astype(BF16)
    lo = (rest - mid.astype(F32)).astype(BF16)
    return hi, mid, lo


def _proj_body(x_ref, w_ref, wt_ref, *refs, n_cast):
    cast_in, (o_ref, ot_ref) = refs[:n_cast], refs[n_cast:n_cast + 2]
    cast_out, scratch = refs[n_cast + 2:2 * n_cast + 2], refs[2 * n_cast + 2:]
    if scratch:
        xb_ref, = scratch

        @pl.when(pl.program_id(1) == 0)
        def _():
            xb_ref[...] = x_ref[...].astype(BF16)
    else:
        xb_ref = x_ref
    o_ref[...] = _dot_nt(xb_ref[...], w_ref[...])
    for src, dst in zip(cast_in, cast_out):
        dst[...] = src[...].astype(BF16)

    @pl.when(pl.program_id(1) == 0)
    def _():
        ot_ref[...] = _dot_nt(xb_ref[...], wt_ref[...])


def _proj(x, w_t, cols, stacked, layer, tm=1024, tn=1024):
    n, d = x.shape
    tm = min(tm, n)
    ni, nj = n // tm, cols // tn
    scratch = [] if x.dtype == BF16 else [pltpu.VMEM((tm, d), BF16)]
    cast_rows = [w.shape[1] // (ni * nj) for w in stacked]
    cast_in = [pl.BlockSpec((None, r, w.shape[2]), lambda i, j: (layer, i * nj + j, 0))
               for w, r in zip(stacked, cast_rows)]
    cast_out = [pl.BlockSpec((r, w.shape[2]), lambda i, j: (i * nj + j, 0)) for w, r in zip(stacked, cast_rows)]
    return pl.pallas_call(
        functools.partial(_proj_body, n_cast=len(stacked)),
        grid=(ni, nj),
        in_specs=[pl.BlockSpec((tm, d), lambda i, j: (i, 0)),
                  pl.BlockSpec((tn, d), lambda i, j: (j, 0)),
                  pl.BlockSpec((LANES, d), lambda i, j: (cols // LANES, 0))] + cast_in,
        out_specs=[pl.BlockSpec((tm, tn), lambda i, j: (i, j)),
                   pl.BlockSpec((tm, LANES), lambda i, j: (i, 0))] + cast_out,
        out_shape=[jax.ShapeDtypeStruct((n, cols), F32), jax.ShapeDtypeStruct((n, LANES), F32)]
        + [jax.ShapeDtypeStruct(w.shape[1:], BF16) for w in stacked],
        scratch_shapes=scratch,
        compiler_params=_params(("parallel", "arbitrary"), 56),
        name="proj",
    )(x, w_t, w_t, *stacked)


def _cast_body(w_ref, o_ref, *, valid_rows):
    w = w_ref[...]
    if valid_rows % w.shape[0]:
        row = pl.program_id(0) * w.shape[0] + lax.broadcasted_iota(jnp.int32, w.shape, 0)
        w = jnp.where(row < valid_rows, w, 0.0)
    o_ref[...] = w.astype(o_ref.dtype)


def _cast_bf16(w, layer, block_rows, pad_rows_to=None):
    _, r, c = w.shape
    out_r = pad_rows_to or r
    return pl.pallas_call(
        functools.partial(_cast_body, valid_rows=r),
        grid=(out_r // block_rows,),
        in_specs=[pl.BlockSpec((None, block_rows, c), lambda i: (layer, i, 0))],
        out_specs=pl.BlockSpec((block_rows, c), lambda i: (i, 0)),
        out_shape=jax.ShapeDtypeStruct((out_r, c), BF16),
        compiler_params=_params(("parallel",), 40),
        name="cast_bf16",
    )(w)


def _s5_body(u_ref, perm_ref, unperm_ref, wb_ref, wc_ref, are_ref, aim_ref, pre_ref, pim_ref, d_ref, gw_ref, gb_ref,
             cast_ref, o_ref, cast_out_ref, sre, sim, hre, him, y_scr, *, sub, steps):
    cast_out_ref[...] = cast_ref[...].astype(BF16)
    n_slabs = u_ref.shape[1] // LANES
    n_sub = u_ref.shape[0] // sub
    sw = S5_SLAB_STATE
    tiles = sw // LANES

    @pl.when(pl.program_id(1) == 0)
    def _():
        hre[...] = jnp.zeros_like(hre)
        him[...] = jnp.zeros_like(him)

    perm = perm_ref[...]
    us = []
    for a in range(n_sub):
        u1, u2, u3 = _split3(u_ref[a * sub:(a + 1) * sub, :])
        u = _dot(perm, u1) + _dot(perm, u2) + _dot(perm, u3)
        us.append(u)
        ub = u.astype(BF16)
        for j in range(n_slabs):
            bu = _dot(ub[:, j * LANES:(j + 1) * LANES], wb_ref[j])
            for k in range(tiles):
                sre[a, j * tiles + k] = bu[:, k * LANES:(k + 1) * LANES]
                sim[a, j * tiles + k] = bu[:, sw + k * LANES:sw + (k + 1) * LANES]

    for a in range(n_sub):
        out_rows = slice(a * sub, (a + 1) * sub)
        for j in range(n_slabs):
            ts = [j * tiles + k for k in range(tiles)]
            cols = [slice(t * LANES, (t + 1) * LANES) for t in ts]

            ar = [jnp.broadcast_to(are_ref[:, c], (SUBLANES, LANES)) for c in cols]
            ai = [jnp.broadcast_to(aim_ref[:, c], (SUBLANES, LANES)) for c in cols]
            hr = [jnp.zeros((SUBLANES, LANES), F32)] * tiles
            hi = [jnp.zeros((SUBLANES, LANES), F32)] * tiles
            for r in range(steps):
                rows = slice(r * SUBLANES, (r + 1) * SUBLANES)
                for k, t in enumerate(ts):
                    hr[k], hi[k] = (ar[k] * hr[k] - ai[k] * hi[k] + sre[a, t, rows, :],
                                    ar[k] * hi[k] + ai[k] * hr[k] + sim[a, t, rows, :])
                    sre[a, t, rows, :] = hr[k]
                    sim[a, t, rows, :] = hi[k]

            cr8, ci8 = [], []
            for k in range(tiles):
                apr = pre_ref[steps - 1:steps, cols[k]]
                api = pim_ref[steps - 1:steps, cols[k]]
                cr = hre[:, cols[k]]
                ci = him[:, cols[k]]
                crs, cis = [], []
                for s in range(SUBLANES):
                    crs.append(cr)
                    cis.append(ci)
                    cr, ci = (apr * cr - api * ci + hr[k][s:s + 1, :], apr * ci + api * cr + hi[k][s:s + 1, :])
                hre[:, cols[k]] = cr
                him[:, cols[k]] = ci
                cr8.append(jnp.concatenate(crs, axis=0))
                ci8.append(jnp.concatenate(cis, axis=0))
            for r in range(steps):
                rows = slice(r * SUBLANES, (r + 1) * SUBLANES)
                for k, t in enumerate(ts):
                    pr = jnp.broadcast_to(pre_ref[r:r + 1, cols[k]], (SUBLANES, LANES))
                    pi = jnp.broadcast_to(pim_ref[r:r + 1, cols[k]], (SUBLANES, LANES))
                    sre[a, t, rows, :] = sre[a, t, rows, :] + (pr * cr8[k] - pi * ci8[k])
                    sim[a, t, rows, :] = sim[a, t, rows, :] + (pr * ci8[k] + pi * cr8[k])

            h_re = jnp.concatenate([sre[a, t] for t in ts], axis=1).astype(BF16)
            h_im = jnp.concatenate([sim[a, t] for t in ts], axis=1).astype(BF16)
            y = _dot(h_re, wc_ref[j, :sw, :]) + _dot(h_im, wc_ref[j, sw:, :])
            lc = slice(j * LANES, (j + 1) * LANES)
            y = y + d_ref[:, lc] * us[a][:, lc]
            y_scr[out_rows, lc] = jax.nn.gelu(y)

        y = y_scr[out_rows, :]
        gate = _dot(y.astype(BF16), gw_ref[...]) + gb_ref[...]
        out = (y * jax.nn.sigmoid(gate)).astype(BF16)
        o_ref[out_rows, :] = _dot(unperm_ref[...], out).astype(o_ref.dtype)


def _s5_prepare(lam_re, lam_im, log_step, b_re, b_im, c_re, c_im, d, steps):
    g, p = lam_re.shape
    h = S5_CH_PER_GROUP
    q = S5_GROUPS_PER_SLAB
    n_slabs = g // q
    step = jnp.exp(log_step.astype(F32))[:, None]
    lr, li = lam_re.astype(F32), lam_im.astype(F32)
    mag = jnp.exp(lr * step)
    a_re, a_im = mag * jnp.cos(li * step), mag * jnp.sin(li * step)
    den = lr * lr + li * li
    f_re = ((a_re - 1.0) * lr + a_im * li) / den
    f_im = (a_im * lr - (a_re - 1.0) * li) / den
    bb_re = f_re[:, :, None] * b_re - f_im[:, :, None] * b_im
    bb_im = f_re[:, :, None] * b_im + f_im[:, :, None] * b_re
    eye = jnp.eye(q, dtype=F32)

    def in_block(m):
        m = m.reshape(n_slabs, q, p, h)
        return jnp.einsum('jqph,qr->jqhrp', m, eye).reshape(n_slabs, q * h, q * p)

    def out_block(m):
        m = m.reshape(n_slabs, q, h, p)
        return jnp.einsum('jqhp,qr->jqprh', m, eye).reshape(n_slabs, q * p, q * h)

    wb = jnp.concatenate([in_block(bb_re), in_block(bb_im)], axis=2).astype(BF16)
    wc = jnp.concatenate([out_block(c_re.astype(F32)), out_block(-c_im.astype(F32))], axis=1).astype(BF16)
    k = jnp.arange(1, steps + 1, dtype=F32)[:, None, None]
    pmag = jnp.exp(lr * step * k)
    ang = li * step * k
    pw_re = (pmag * jnp.cos(ang)).reshape(steps, g * p)
    pw_im = (pmag * jnp.sin(ang)).reshape(steps, g * p)
    return (wb, wc, a_re.reshape(1, g * p), a_im.reshape(1, g * p), pw_re, pw_im,
            d.astype(F32).reshape(1, g * h))


def _s5(proj, bsz, seq, prep, glu_w, glu_b, sub, cast_w, layer, n_sub=4):
    wb, wc, a_re, a_im, pw_re, pw_im, d = prep
    tl = min(sub * n_sub, seq)
    steps = sub // SUBLANES
    width = d.shape[1]
    states = pw_re.shape[1]
    nl = seq // tl
    full = lambda a: pl.BlockSpec(a.shape, lambda b, l: (0,) * a.ndim)
    gw = glu_w.astype(BF16)
    gb = glu_b.astype(F32).reshape(1, width)
    dest = jnp.arange(sub)
    src = (dest % SUBLANES) * steps + dest // SUBLANES
    perm = (src[:, None] == jnp.arange(sub)[None, :]).astype(BF16)
    unperm = perm.T
    state_scratch = pltpu.VMEM((tl // sub, states // LANES, sub, LANES), F32)
    cast_rows = cast_w.shape[1] // (bsz * nl)
    return pl.pallas_call(
        functools.partial(_s5_body, sub=sub, steps=steps),
        grid=(bsz, nl),
        in_specs=[pl.BlockSpec((tl, width), lambda b, l: (b * nl + l, 0)), full(perm), full(unperm),
                  full(wb), full(wc), full(a_re), full(a_im), full(pw_re), full(pw_im), full(d), full(gw), full(gb),
                  pl.BlockSpec((None, cast_rows, cast_w.shape[2]), lambda b, l: (layer, b * nl + l, 0))],
        out_specs=[pl.BlockSpec((tl, width), lambda b, l: (b * nl + l, 0)),
                   pl.BlockSpec((cast_rows, cast_w.shape[2]), lambda b, l: (b * nl + l, 0))],
        out_shape=[jax.ShapeDtypeStruct((bsz * seq, width), BF16), jax.ShapeDtypeStruct(cast_w.shape[1:], BF16)],
        scratch_shapes=[state_scratch, state_scratch,
                        pltpu.VMEM((1, states), F32), pltpu.VMEM((1, states), F32),
                        pltpu.VMEM((tl, width), F32)],
        compiler_params=_params(("arbitrary", "arbitrary"), 40),
        name="s5",
    )(proj, perm, unperm, wb, wc, a_re, a_im, pw_re, pw_im, d, gw, gb, cast_w)


def _sgu_body(zu_ref, zv_ref, g_ref, b_ref, w_ref, bias_ref, o_ref, *, tl):
    width = zu_ref.shape[1]
    u = jax.nn.gelu(zu_ref[...])
    v = _layer_norm(jax.nn.gelu(zv_ref[...]), g_ref[...], b_ref[...])
    vb = v.astype(BF16)
    row = lax.broadcasted_iota(jnp.int32, (2 * SGU_CHUNK, SGU_CHUNK), 0)
    col = lax.broadcasted_iota(jnp.int32, (2 * SGU_CHUNK, SGU_CHUNK), 1)
    causal = col <= (row % SGU_CHUNK)
    lane = lax.broadcasted_iota(jnp.int32, (SGU_CHUNK, LANES), 1)
    first = lane < SGU_HEAD_DIM
    for j in range(width // LANES):
        w = jnp.where(causal, w_ref[j], 0.0).astype(BF16)
        bias = bias_ref[j]
        for c in range(tl // SGU_CHUNK):
            rs = slice(c * SGU_CHUNK, (c + 1) * SGU_CHUNK)
            cs = slice(j * LANES, (j + 1) * LANES)
            r = _dot(w, vb[rs, cs])
            mixed = jnp.where(first, r[:SGU_CHUNK], r[SGU_CHUNK:]) + bias
            o_ref[rs, cs] = (u[rs, cs] * mixed).astype(o_ref.dtype)


def _sgu(proj, n, width, norm_g, norm_b, w_s, b_s, tl=512):
    heads = w_s.shape[0]
    tl = min(tl, n)
    w_pairs = w_s.astype(F32).reshape(heads // 2, 2 * SGU_CHUNK, SGU_CHUNK)
    bias = jnp.repeat(b_s.astype(F32).T, SGU_HEAD_DIM, axis=1)
    bias = bias.reshape(SGU_CHUNK, heads // 2, LANES).transpose(1, 0, 2)
    g = norm_g.astype(F32).reshape(1, width)
    b = norm_b.astype(F32).reshape(1, width)
    full = lambda a: pl.BlockSpec(a.shape, lambda i: (0,) * a.ndim)
    return pl.pallas_call(
        functools.partial(_sgu_body, tl=tl),
        grid=(n // tl,),
        in_specs=[pl.BlockSpec((tl, width), lambda i: (i, 1)),
                  pl.BlockSpec((tl, width), lambda i: (i, 2)),
                  full(g), full(b), full(w_pairs), full(bias)],
        out_specs=pl.BlockSpec((tl, width), lambda i: (i, 0)),
        out_shape=jax.ShapeDtypeStruct((n, width), BF16),
        compiler_params=_params(("parallel",), 40),
        name="sgu",
    )(proj, proj, g, b, w_pairs, bias)


def _pool_body(p_ref, w_ref, sc_ref, o_ref, ext, *, tl):
    l = pl.program_id(1)

    @pl.when(l == 0)
    def _():
        ext[0:POOL_HALO, :] = jnp.zeros((POOL_HALO, ext.shape[1]), F32)

    @pl.when(l > 0)
    def _():
        ext[0:POOL_HALO, :] = ext[tl:tl + POOL_HALO, :]

    ext[POOL_HALO:POOL_HALO + tl, :] = p_ref[...]
    pos = l * tl + lax.broadcasted_iota(jnp.int32, (tl, LANES), 0)
    for gi, win in enumerate(POOL_WINDOWS):
        cs = slice(gi * LANES, (gi + 1) * LANES)
        x = ext[POOL_HALO:POOL_HALO + tl, cs]
        acc = ext[:, cs]
        span = 1
        while span < win:
            acc = acc + pltpu.roll(acc, span, 0)
            span *= 2
        acc = acc[POOL_HALO:, :]
        count = jnp.minimum(pos + 1, win).astype(F32)
        pooled = acc / count - x
        y = _dot(pooled.astype(BF16), w_ref[gi]) * sc_ref[:, cs]
        o_ref[:, cs] = y.astype(o_ref.dtype)


def _pool(proj, bsz, seq, width, w_pool, scale, col_block, tl=512):
    tl = min(tl, seq)
    nl = seq // tl
    w = w_pool.astype(BF16)
    sc = scale.astype(F32).reshape(1, width)
    return pl.pallas_call(
        functools.partial(_pool_body, tl=tl),
        grid=(bsz, nl),
        in_specs=[pl.BlockSpec((tl, width), lambda b, l: (b * nl + l, col_block)),
                  pl.BlockSpec(w.shape, lambda b, l: (0, 0, 0)),
                  pl.BlockSpec(sc.shape, lambda b, l: (0, 0))],
        out_specs=pl.BlockSpec((tl, width), lambda b, l: (b * nl + l, 0)),
        out_shape=jax.ShapeDtypeStruct((bsz * seq, width), BF16),
        scratch_shapes=[pltpu.VMEM((tl + POOL_HALO, width), F32)],
        compiler_params=_params(("arbitrary", "arbitrary"), 40),
        name="pool",
    )(proj, w, sc)


def _dn_prep_body(q_ref, k_ref, v_ref, tail_ref, cw_ref, alog_ref, dtb_ref, cast_ref,
                  vk_ref, qg_ref, kt_ref, qk_ref, low_ref, gl_ref, cast_out_ref, ext, *, tl, heads):
    cast_out_ref[...] = cast_ref[...].astype(BF16)
    l = pl.program_id(1)
    width = heads * DN_HEAD_DIM
    c = DN_CHUNK

    @pl.when(l == 0)
    def _():
        ext[:, 0:DN_HALO, :] = jnp.zeros((3, DN_HALO, width), F32)

    @pl.when(l > 0)
    def _():
        ext[:, 0:DN_HALO, :] = ext[:, tl:tl + DN_HALO, :]

    mixed = []
    for i, ref in enumerate((q_ref, k_ref, v_ref)):
        ext[i, DN_HALO:DN_HALO + tl, :] = ref[...]
        e = ext[i]
        acc = e * cw_ref[0:1, i * width:(i + 1) * width]
        for j in range(1, DN_CONV):
            acc = pltpu.roll(acc, 1, 0) + e * cw_ref[j:j + 1, i * width:(i + 1) * width]
        acc = acc[DN_HALO:, :]
        mixed.append(acc * jax.nn.sigmoid(acc))
    qc, kc, vc = mixed

    tail = tail_ref[...]
    sp_in = tail + dtb_ref[...]
    softplus = jnp.maximum(sp_in, 0.0) + jnp.log1p(jnp.exp(-jnp.abs(sp_in)))
    g_all = -jnp.exp(alog_ref[...]) * softplus
    beta_all = jax.nn.sigmoid(tail)

    ri = lax.broadcasted_iota(jnp.int32, (c, c), 0)
    ci = lax.broadcasted_iota(jnp.int32, (c, c), 1)
    causal = ri >= ci
    strict = ri > ci
    tri = causal.astype(BF16)
    sel = (lax.broadcasted_iota(jnp.int32, (heads * c, LANES), 0) // c
           == lax.broadcasted_iota(jnp.int32, (heads * c, LANES), 1)).astype(BF16)
    chunks = range(tl // c)

    gc = []
    for n in chunks:
        g1, g2, g3 = _split3(g_all[n * c:(n + 1) * c, :])
        gc.append(_dot(tri, g1) + _dot(tri, g2) + _dot(tri, g3))
    grow = []
    for n in chunks:
        g1, g2, g3 = _split3(gc[n])
        grow.append(_dot_nt(sel, g1) + _dot_nt(sel, g2) + _dot_nt(sel, g3))
    for n in chunks:
        rs = slice(n * c, (n + 1) * c)
        qn, kn, kbeta, kq, kkt = {}, {}, {}, {}, {}
        for h in range(heads):
            hs = slice(h * DN_HEAD_DIM, (h + 1) * DN_HEAD_DIM)
            q = qc[rs, hs]
            k = kc[rs, hs]
            q = q * lax.rsqrt(jnp.sum(q * q, axis=-1, keepdims=True) + L2_EPS) * (DN_HEAD_DIM ** -0.5)
            k = k * lax.rsqrt(jnp.sum(k * k, axis=-1, keepdims=True) + L2_EPS)
            kb = k * beta_all[rs, heads + h:heads + h + 1]
            qn[n, h], kn[n, h], kbeta[n, h] = q, k, kb
            if h % 2 == 0:
                kq[n, h] = _dot_nt(jnp.concatenate([kb.astype(BF16), q.astype(BF16)], axis=0), k.astype(BF16))
            else:
                kq[n, h] = _dot_nt(q.astype(BF16), k.astype(BF16))
                kkt[n, h] = _dot_nt(k.astype(BF16), kb.astype(BF16))
        for h in range(heads):
            hs = slice(h * DN_HEAD_DIM, (h + 1) * DN_HEAD_DIM)
            ms = slice(h * c, (h + 1) * c)
            gcol = gc[n][:, h:h + 1]
            glast = gc[n][c - 1:c, h:h + 1]
            decay = jnp.exp(-jnp.abs(gcol - grow[n][ms, :]))
            egc = jnp.exp(gcol)
            if h % 2 == 0:
                packed = jnp.where(strict, kq[n, h][:c] * decay, 0.0)
                qk_ref[rs, ms] = jnp.where(causal, kq[n, h][c:] * decay, 0.0)
            else:
                low_ref[rs, (h // 2) * c:(h // 2 + 1) * c] = jnp.where(ri < ci, kkt[n, h] * decay, packed)
                qk_ref[rs, ms] = jnp.where(causal, kq[n, h] * decay, 0.0)
            beta = beta_all[rs, heads + h:heads + h + 1]
            vk_ref[rs, 2 * h * DN_HEAD_DIM:(2 * h + 1) * DN_HEAD_DIM] = (vc[rs, hs] * beta).astype(BF16)
            vk_ref[rs, (2 * h + 1) * DN_HEAD_DIM:(2 * h + 2) * DN_HEAD_DIM] = (kbeta[n, h] * egc).astype(BF16)
            qg_ref[rs, hs] = (qn[n, h] * egc).astype(BF16)
            kt_ref[rs, hs] = (kn[n, h] * jnp.exp(glast - gcol)).astype(BF16)
            gl_ref[n * heads + h:n * heads + h + 1, :] = jnp.broadcast_to(jnp.exp(glast), (1, LANES))


def _tri_inv_body(l_ref, t_ref, lo_scr, up_scr):
    c, _, lanes = l_ref.shape
    groups = c // SUBLANES
    sub = lax.broadcasted_iota(jnp.int32, (SUBLANES, lanes), 0)
    zero = jnp.zeros((SUBLANES, lanes), F32)
    grp = lambda g: slice(g * SUBLANES, (g + 1) * SUBLANES)
    pick = lambda v, r: jnp.broadcast_to(v[r:r + 1, :], (SUBLANES, lanes))

    for j in range(c):
        d = l_ref[j, grp(j // SUBLANES), :]
        lo_scr[j] = jnp.where(sub > j % SUBLANES, d, 0.0)
        up_scr[j] = jnp.where(sub < j % SUBLANES, d, 0.0)

    for col in range(c):
        dg, dr = col // SUBLANES, col % SUBLANES
        unit = (sub == dr).astype(F32)
        t = {g: zero for g in range(dg, groups)}
        t[dg] = unit
        for j in range(col, c - 1):
            jg = j // SUBLANES
            b = pick(t[jg], j % SUBLANES)
            t[jg] = t[jg] - lo_scr[j] * b
            for g in range(jg + 1, groups):
                t[g] = t[g] - l_ref[j, grp(g), :] * b
        u = {g: zero for g in range(dg + 1)}
        u[dg] = unit
        for j in range(col, 0, -1):
            jg = j // SUBLANES
            b = pick(u[jg], j % SUBLANES)
            u[jg] = u[jg] - up_scr[j] * b
            for g in range(jg):
                u[g] = u[g] - l_ref[j, grp(g), :] * b
        for g in range(groups):
            if g < dg:
                t_ref[col, grp(g), :] = u[g]
            elif g > dg:
                t_ref[col, grp(g), :] = t[g]
            else:
                t_ref[col, grp(g), :] = jnp.where(sub > dr, t[g], jnp.where(sub < dr, u[g], 0.0))


def _dn_rec_body(t_ref, vk_ref, qg_ref, kt_ref, qk_ref, gl_ref, gate_ref, ng_ref, o_ref, state,
                 *, tl, heads):
    c = DN_CHUNK
    dk = DN_HEAD_DIM

    @pl.when(pl.program_id(1) == 0)
    def _():
        state[...] = jnp.zeros_like(state)

    chunks = range(tl // c)
    rows = lambda n: slice(n * c, (n + 1) * c)
    hcol = lambda h: slice(h * dk, (h + 1) * dk)
    mcol = lambda h: slice(h * c, (h + 1) * c)
    ri = lax.broadcasted_iota(jnp.int32, (c, c), 0)
    ci = lax.broadcasted_iota(jnp.int32, (c, c), 1)

    uw, uwb, kuw = {}, {}, {}
    for n in chunks:
        for h in range(heads):
            packed = t_ref[rows(n), mcol(h // 2)]
            vk = vk_ref[rows(n), 2 * h * dk:2 * (h + 1) * dk]
            if h % 2 == 0:
                uw[n, h] = _dot(jnp.where(ri > ci, packed, 0.0).astype(BF16), vk) + vk.astype(F32)
            else:
                uw[n, h] = _dot_tn(jnp.where(ri < ci, packed, 0.0).astype(BF16), vk) + vk.astype(F32)
            uwb[n, h] = uw[n, h].astype(BF16)
    for n in chunks:
        for h in range(heads):
            kuw[n, h] = _dot_tn(kt_ref[rows(n), hcol(h)], uwb[n, h])

    def emit(n, h, wq, v_bf16):
        o = wq[c:] + _dot(qk_ref[rows(n), mcol(h)].astype(BF16), v_bf16)
        o = o * lax.rsqrt(jnp.mean(o * o, axis=-1, keepdims=True) + RMS_EPS) * ng_ref[...]
        gt = gate_ref[rows(n), hcol(h)]
        o_ref[rows(n), hcol(h)] = (o * (gt * jax.nn.sigmoid(gt))).astype(o_ref.dtype)

    s = [state[h] for h in range(heads)]
    pending = []
    for n in chunks:
        sb = [s[h].astype(BF16) for h in range(heads)]
        for h in range(heads):
            s[h] = (s[h] * gl_ref[n * heads + h:n * heads + h + 1, :] + kuw[n, h][:, :dk]
                    - _dot(kuw[n, h][:, dk:].astype(BF16), sb[h]))
        wq = []
        for h in range(heads):
            lhs = jnp.concatenate([uwb[n, h][:, dk:], qg_ref[rows(n), hcol(h)]], axis=0)
            wq.append(_dot(lhs, sb[h]))
        for args in pending:
            emit(*args)
        pending = [(n, h, wq[h], (uw[n, h][:, :dk] - wq[h][:c]).astype(BF16)) for h in range(heads)]
    for args in pending:
        emit(*args)
    for h in range(heads):
        state[h] = s[h]


def _dn(proj, tail, bsz, seq, width, col_block, conv_w, a_log, dt_bias, norm_g, cast_w, layer, tl=512):
    heads = width // DN_HEAD_DIM
    c = DN_CHUNK
    n = bsz * seq
    tl = min(tl, seq)
    nl = seq // tl
    chunks = n // c
    cw = conv_w.astype(F32)
    pad = lambda a: jnp.pad(a.astype(F32), (0, LANES - heads)).reshape(1, LANES)
    alog = pad(a_log)
    dtb = pad(dt_bias)
    ng = norm_g.astype(F32).reshape(1, DN_HEAD_DIM)
    full = lambda a: pl.BlockSpec(a.shape, lambda b, l: (0,) * a.ndim)
    slab = lambda k: pl.BlockSpec((tl, width), lambda b, l: (b * nl + l, col_block + k))
    rows = lambda wd: pl.BlockSpec((tl, wd), lambda b, l: (b * nl + l, 0))
    gl_spec = pl.BlockSpec((tl // c * heads, LANES), lambda b, l: (b * nl + l, 0))
    wide = jax.ShapeDtypeStruct((n, width), BF16)
    mats = jax.ShapeDtypeStruct((n, heads * c), F32)
    pairs = heads // 2
    packed = jax.ShapeDtypeStruct((n, pairs * c), F32)
    cast_rows = cast_w.shape[1] // (bsz * nl)
    cast_in = pl.BlockSpec((None, cast_rows, cast_w.shape[2]), lambda b, l: (layer, b * nl + l, 0))
    cast_out = pl.BlockSpec((cast_rows, cast_w.shape[2]), lambda b, l: (b * nl + l, 0))
    vk, qg, kt, qk, low, gl, cast_b = pl.pallas_call(
        functools.partial(_dn_prep_body, tl=tl, heads=heads),
        grid=(bsz, nl),
        in_specs=[slab(0), slab(1), slab(2), rows(LANES), full(cw), full(alog), full(dtb), cast_in],
        out_specs=[rows(2 * width)] + [rows(width)] * 2 + [rows(heads * c), rows(pairs * c), gl_spec, cast_out],
        out_shape=[jax.ShapeDtypeStruct((n, 2 * width), BF16)] + [wide] * 2 + [mats, packed]
        + [jax.ShapeDtypeStruct((chunks * heads, LANES), F32), jax.ShapeDtypeStruct(cast_w.shape[1:], BF16)],
        scratch_shapes=[pltpu.VMEM((3, tl + DN_HALO, width), F32)],
        compiler_params=_params(("arbitrary", "arbitrary"), 40),
        name="deltanet_prep",
    )(proj, proj, proj, tail, cw, alog, dtb, cast_w)

    lane_block = min(LANES, chunks)
    low_t = low.reshape(chunks, c, pairs * c).transpose(2, 1, 0)
    inv_spec = pl.BlockSpec((c, c, lane_block), lambda p, cb: (p, 0, cb))
    t_t = pl.pallas_call(
        _tri_inv_body,
        grid=(pairs, chunks // lane_block),
        in_specs=[inv_spec],
        out_specs=inv_spec,
        out_shape=jax.ShapeDtypeStruct((pairs * c, c, chunks), F32),
        scratch_shapes=[pltpu.VMEM((c, SUBLANES, lane_block), F32), pltpu.VMEM((c, SUBLANES, lane_block), F32)],
        compiler_params=_params(("parallel", "parallel"), 40),
        name="deltanet_tri_inv",
    )(low_t)
    t_mat = t_t.transpose(2, 1, 0).reshape(n, pairs * c)

    y = pl.pallas_call(
        functools.partial(_dn_rec_body, tl=tl, heads=heads),
        grid=(bsz, nl),
        in_specs=[rows(pairs * c), rows(2 * width)] + [rows(width)] * 2
        + [rows(heads * c), gl_spec, slab(3), full(ng)],
        out_specs=rows(width),
        out_shape=wide,
        scratch_shapes=[pltpu.VMEM((heads, DN_HEAD_DIM, DN_HEAD_DIM), F32)],
        compiler_params=_params(("arbitrary", "arbitrary"), 40),
        name="deltanet_rec",
    )(t_mat, vk, qg, kt, qk, gl, proj, ng)
    return y, cast_b


def _out_body(y0_ref, y1_ref, y2_ref, y3_ref, w_ref, x_ref, g_ref, b_ref, o_ref, ob_ref, *, alpha):
    width = y0_ref.shape[1]
    half = x_ref.shape[0] // 2
    accs = []
    for rows in (slice(0, half), slice(half, 2 * half)):
        mixed = jnp.concatenate([ref[rows, :] for ref in (y0_ref, y1_ref, y2_ref, y3_ref)], axis=1)
        accs.append((rows, alpha * x_ref[rows, :] + _dot(mixed, w_ref[...])))
    for rows, acc in accs:
        y = _layer_norm(acc, g_ref[...], b_ref[...])
        o_ref[rows, :] = y
        ob_ref[rows, :] = y.astype(BF16)


def _out_proj(ys, w_out, x, g, b, alpha, tm=512):
    n, d = x.shape
    width = ys[0].shape[1]
    tm = min(tm, n)
    row = lambda wd: pl.BlockSpec((tm, wd), lambda i: (i, 0))
    full = lambda a: pl.BlockSpec(a.shape, lambda i: (0,) * a.ndim)
    g = g.astype(F32).reshape(1, d)
    b = b.astype(F32).reshape(1, d)
    return pl.pallas_call(
        functools.partial(_out_body, alpha=alpha),
        grid=(n // tm,),
        in_specs=[row(width)] * 4 + [full(w_out), row(d), full(g), full(b)],
        out_specs=[row(d), row(d)],
        out_shape=[jax.ShapeDtypeStruct((n, d), F32), jax.ShapeDtypeStruct((n, d), BF16)],
        compiler_params=_params(("parallel",), 56),
        name="out_proj_ln",
    )(*ys, w_out, x, g, b)


def _ffn_body(xb_ref, wu_ref, wd_ref, x_ref, g_ref, b_ref, o_ref, ob_ref, acc, *, alpha):
    f = pl.program_id(1)

    @pl.when(f == 0)
    def _():
        acc[...] = alpha * x_ref[...]

    hidden = jnp.maximum(_dot(xb_ref[...], wu_ref[...]), 0.0)
    acc[...] += _dot((hidden * hidden).astype(BF16), wd_ref[...])

    @pl.when(f == pl.num_programs(1) - 1)
    def _():
        y = _layer_norm(acc[...], g_ref[...], b_ref[...])
        o_ref[...] = y
        ob_ref[...] = y.astype(BF16)


def _ffn(xb, x, w_up, w_down, g, b, alpha, tm=512, tf=1024):
    n, d = x.shape
    ff = w_up.shape[1]
    tm = min(tm, n)
    g = g.astype(F32).reshape(1, d)
    b = b.astype(F32).reshape(1, d)
    row = pl.BlockSpec((tm, d), lambda i, f: (i, 0))
    vec = pl.BlockSpec((1, d), lambda i, f: (0, 0))
    return pl.pallas_call(
        functools.partial(_ffn_body, alpha=alpha),
        grid=(n // tm, ff // tf),
        in_specs=[row, pl.BlockSpec((d, tf), lambda i, f: (0, f)), pl.BlockSpec((tf, d), lambda i, f: (f, 0)),
                  row, vec, vec],
        out_specs=[row, row],
        out_shape=[jax.ShapeDtypeStruct((n, d), F32), jax.ShapeDtypeStruct((n, d), BF16)],
        scratch_shapes=[pltpu.VMEM((tm, d), F32)],
        compiler_params=_params(("parallel", "arbitrary"), 56),
        name="ffn_ln",
    )(xb, w_up, w_down, x, g, b)


def kernel(x, w_in, s5_lambda_re, s5_lambda_im, s5_log_step, s5_b_re, s5_b_im, s5_c_re, s5_c_im, s5_d, s5_glu_w, s5_glu_b, sgu_norm_g, sgu_norm_b, sgu_w, sgu_b, pool_w, pool_scale, dn_conv_w, dn_a_log, dn_dt_bias, dn_norm_g, w_out, ln1_g, ln1_b, w_up, w_down, ln2_g, ln2_b):
    bsz, seq, d = x.shape
    depth = w_in.shape[0]
    n = bsz * seq
    width = s5_glu_w.shape[1]
    main_cols = 8 * width
    alpha = (2 * depth) ** 0.25
    s5_sub = min(256, seq)

    xf = x.reshape(n, d).astype(F32)
    xb = xf
    w_in_t = jnp.swapaxes(w_in, 1, 2)
    for i in range(depth):
        w_in_b = _cast_bf16(w_in_t, i, 3 * LANES, pad_rows_to=main_cols + LANES)
        proj, tail, w_out_b = _proj(xb, w_in_b, main_cols, (w_out,), i)
        prep = _s5_prepare(s5_lambda_re[i], s5_lambda_im[i], s5_log_step[i], s5_b_re[i], s5_b_im[i],
                           s5_c_re[i], s5_c_im[i], s5_d[i], s5_sub // SUBLANES)
        y_s5, w_up_b = _s5(proj, bsz, seq, prep, s5_glu_w[i], s5_glu_b[i], s5_sub, w_up, i)
        y_sgu = _sgu(proj, n, width, sgu_norm_g[i], sgu_norm_b[i], sgu_w[i], sgu_b[i])
        y_pool = _pool(proj, bsz, seq, width, pool_w[i], pool_scale[i], col_block=3)
        y_dn, w_down_b = _dn(proj, tail, bsz, seq, width, 4, dn_conv_w[i], dn_a_log[i], dn_dt_bias[i], dn_norm_g[i],
                             w_down, i)
        xf, xb = _out_proj((y_s5, y_sgu, y_pool, y_dn), w_out_b, xf, ln1_g[i], ln1_b[i], alpha)
        xf, xb = _ffn(xb, xf, w_up_b, w_down_b, ln2_g[i], ln2_b[i], alpha)
    return xf.reshape(bsz, seq, d).astype(x.dtype)
```
